```python
import jax, jax.numpy as jnp
from jax import lax
import numpy as np

D_MODEL = 2048
BATCH = 2
SEQ = 8192
DEPTH = 4

MEM_LEN = 256
HEAD_DIM = 128
N_MIX_HEADS = 12
N_MEM_HEADS = 4
MIX_WIDTH = N_MIX_HEADS * HEAD_DIM
MEM_WIDTH = N_MEM_HEADS * HEAD_DIM
CAT_WIDTH = MIX_WIDTH + MEM_WIDTH

Q_LORA = 512
KV_LORA = 512
QK_NOPE = 128
QK_ROPE = 64
V_HEAD = HEAD_DIM
ROPE_THETA = 10000.0
Q_BLOCK = 128
MLA_MIX_IN = Q_LORA + KV_LORA + QK_ROPE
MLA_IN = MLA_MIX_IN + MEM_WIDTH

CONV_WIDTH = 4
CHUNK = 64
GDN_MIX_IN = 4 * MIX_WIDTH + 2 * N_MIX_HEADS
GDN_IN = GDN_MIX_IN + MEM_WIDTH

D_FF = 4 * D_MODEL
ALPHA = (2 * DEPTH) ** 0.25
BETA_INIT = (8 * DEPTH) ** -0.25
N_MLA = (DEPTH + 1) // 2
N_GDN = DEPTH // 2
LN_EPS = 1e-5
RMS_EPS = 1e-6

kernel_name = "hybrid_mla_gdn_deepnorm_memory"


def layer_norm(x, g, b):
    xf = x.astype(jnp.float32)
    mu = jnp.mean(xf, -1, keepdims=True)
    var = jnp.mean(jnp.square(xf - mu), -1, keepdims=True)
    return ((xf - mu) * lax.rsqrt(var + LN_EPS) * g + b).astype(x.dtype)


def rms_norm(x, g):
    xf = x.astype(jnp.float32)
    return (xf * lax.rsqrt(jnp.mean(xf * xf, -1, keepdims=True) + RMS_EPS) * g).astype(x.dtype)


def l2_normalize(x):
    return x * lax.rsqrt(jnp.sum(x * x, -1, keepdims=True) + 1e-6)


def rope(x, cos, sin):
    half = x.shape[-1] // 2
    x1, x2 = x[..., :half], x[..., half:]
    return jnp.concatenate([x1 * cos - x2 * sin, x2 * cos + x1 * sin], -1)


def causal_block_attention(q, k, v):
    B_, S_, H, Dq = q.shape
    nb = S_ // Q_BLOCK
    qb = jnp.moveaxis(q.reshape(B_, nb, Q_BLOCK, H, Dq), 1, 0)
    kpos = jnp.arange(S_)
    scale = Dq ** -0.5

    def one_block(args):
        qi, bi = args
        s = jnp.einsum('bqhd,bkhd->bhqk', qi, k).astype(jnp.float32) * scale
        qpos = bi * Q_BLOCK + jnp.arange(Q_BLOCK)
        s = jnp.where(kpos[None, :] <= qpos[:, None], s, -jnp.inf)
        p = jax.nn.softmax(s, axis=-1).astype(v.dtype)
        return jnp.einsum('bhqk,bkhd->bqhd', p, v)

    o = lax.map(one_block, (qb, jnp.arange(nb)))
    return jnp.moveaxis(o, 0, 1).reshape(B_, S_, H, v.shape[-1])


def memory_attention(q, mem_kv):
    B_, M_, _ = mem_kv.shape
    k = mem_kv[..., :MEM_WIDTH].reshape(B_, M_, N_MEM_HEADS, HEAD_DIM)
    v = mem_kv[..., MEM_WIDTH:].reshape(B_, M_, N_MEM_HEADS, HEAD_DIM)
    s = jnp.einsum('bshd,bmhd->bhsm', q, k).astype(jnp.float32) * HEAD_DIM ** -0.5
    p = jax.nn.softmax(s, axis=-1).astype(v.dtype)
    o = jnp.einsum('bhsm,bmhd->bshd', p, v)
    return o.reshape(q.shape[0], q.shape[1], MEM_WIDTH)


def mla_mixer(h, cos, sin, q_norm, w_uq, kv_norm, w_ukv):
    B_, S_, _ = h.shape
    c_q = h[..., :Q_LORA]
    c_kv = h[..., Q_LORA:Q_LORA + KV_LORA]
    k_r = h[..., Q_LORA + KV_LORA:]
    q = (rms_norm(c_q, q_norm) @ w_uq).reshape(B_, S_, N_MIX_HEADS, QK_NOPE + QK_ROPE)
    q = jnp.concatenate([q[..., :QK_NOPE], rope(q[..., QK_NOPE:], cos, sin)], -1)
    kv = (rms_norm(c_kv, kv_norm) @ w_ukv).reshape(B_, S_, N_MIX_HEADS, QK_NOPE + V_HEAD)
    k_r = rope(k_r[:, :, None, :], cos, sin)
    k = jnp.concatenate([kv[..., :QK_NOPE], jnp.broadcast_to(k_r, (B_, S_, N_MIX_HEADS, QK_ROPE))], -1)
    v = kv[..., QK_NOPE:]
    return causal_block_attention(q, k, v).reshape(B_, S_, MIX_WIDTH)


def causal_depthwise_conv(x, w):
    C = x.shape[-1]
    return lax.conv_general_dilated(
        x, w[:, None, :].astype(x.dtype), window_strides=(1,),
        padding=[(CONV_WIDTH - 1, 0)], dimension_numbers=('NWC', 'WIO', 'NWC'),
        feature_group_count=C)


def gated_delta_chunked(q, k, v, g, beta):
    B_, S_, H, Dk = q.shape
    Dv = v.shape[-1]
    N = S_ // CHUNK

    def to_chunks(t):
        return jnp.moveaxis(t.reshape((B_, N, CHUNK) + t.shape[2:]), 2, 3)

    q = to_chunks(q * Dk ** -0.5)
    k = to_chunks(k)
    v = to_chunks(v)
    beta = to_chunks(beta)
    gc = jnp.cumsum(to_chunks(g), axis=-1)

    tril = jnp.tril(jnp.ones((CHUNK, CHUNK), bool))
    strict = jnp.tril(jnp.ones((CHUNK, CHUNK), bool), k=-1)
    diff = gc[..., :, None] - gc[..., None, :]
    decay = jnp.where(tril, jnp.exp(jnp.where(tril, diff, 0.0)), 0.0)

    kb = k * beta[..., None]
    L = jnp.where(strict, jnp.einsum('bnhcd,bnhed->bnhce', kb, k) * decay, 0.0)
    a = L + jnp.eye(CHUNK, dtype=jnp.float32)
    rhs = jnp.concatenate([v * beta[..., None], kb * jnp.exp(gc)[..., None]], -1)
    sol = lax.linalg.triangular_solve(a, rhs, left_side=True, lower=True, unit_diagonal=True)
    u, w = sol[..., :Dv], sol[..., Dv:]

    a_qk = jnp.where(tril, jnp.einsum('bnhcd,bnhed->bnhce', q, k) * decay, 0.0)
    q_dec = q * jnp.exp(gc)[..., None]
    k_dec = k * jnp.exp(gc[..., -1:] - gc)[..., None]
    g_last = jnp.exp(gc[..., -1])

    def step(state, inp):
        qd, kd, uu, ww, aqk, gl = inp
        v_new = uu - jnp.einsum('bhcd,bhde->bhce', ww, state)
        o = jnp.einsum('bhcd,bhde->bhce', qd, state) + jnp.einsum('bhce,bhef->bhcf', aqk, v_new)
        state = state * gl[..., None, None] + jnp.einsum('bhcd,bhce->bhde', kd, v_new)
        return state, o

    xs = tuple(jnp.moveaxis(t, 1, 0) for t in (q_dec, k_dec, u, w, a_qk, g_last))
    state0 = jnp.zeros((B_, H, Dk, Dv), jnp.float32)
    _, o = lax.scan(step, state0, xs)
    return jnp.transpose(o, (1, 0, 3, 2, 4)).reshape(B_, S_, H, Dv)


def gdn_mixer(h, conv_w, a_log, dt_bias, o_norm):
    B_, S_, _ = h.shape
    qkv = jax.nn.silu(causal_depthwise_conv(h[..., :3 * MIX_WIDTH], conv_w)).astype(jnp.float32)
    z = h[..., 3 * MIX_WIDTH:4 * MIX_WIDTH].astype(jnp.float32)
    a_in = h[..., 4 * MIX_WIDTH:4 * MIX_WIDTH + N_MIX_HEADS].astype(jnp.float32)
    b_in = h[..., 4 * MIX_WIDTH + N_MIX_HEADS:].astype(jnp.float32)
    shp = (B_, S_, N_MIX_HEADS, HEAD_DIM)
    q = l2_normalize(qkv[..., :MIX_WIDTH].reshape(shp))
    k = l2_normalize(qkv[..., MIX_WIDTH:2 * MIX_WIDTH].reshape(shp))
    v = qkv[..., 2 * MIX_WIDTH:].reshape(shp)
    g = -jnp.exp(a_log.astype(jnp.float32)) * jax.nn.softplus(a_in + dt_bias.astype(jnp.float32))
    beta = jax.nn.sigmoid(b_in)
    o = gated_delta_chunked(q, k, v, g, beta)
    o = rms_norm(o, o_norm.astype(jnp.float32)) * jax.nn.silu(z.reshape(shp))
    return o.reshape(B_, S_, MIX_WIDTH).astype(h.dtype)


def setup_inputs(seed: int = 0) -> dict:
    key = jax.random.key(seed)
    ks = jax.random.split(key, 24)
    f32 = jnp.float32
    nrm = lambda k, shape, scale: jax.random.normal(k, shape, f32) * scale
    gain = lambda k, shape: 1.0 + 0.02 * jax.random.normal(k, shape, f32)
    offset = jax.random.randint(ks[2], (BATCH, 1), 0, 1024, dtype=jnp.int32)
    positions = (offset + jnp.arange(SEQ, dtype=jnp.int32)[None, :]).astype(jnp.int32)
    dt = jnp.exp(jax.random.uniform(ks[11], (N_GDN, N_MIX_HEADS), f32, np.log(1e-3), np.log(1e-1)))
    return {
        "x": jax.random.normal(ks[0], (BATCH, SEQ, D_MODEL), f32),
        "mem": jax.random.normal(ks[1], (BATCH, MEM_LEN, D_MODEL), f32),
        "positions": positions,
        "mla_w_in": nrm(ks[3], (N_MLA, D_MODEL, MLA_IN), D_MODEL ** -0.5),
        "mla_q_norm": gain(ks[4], (N_MLA, Q_LORA)),
        "mla_w_uq": nrm(ks[5], (N_MLA, Q_LORA, N_MIX_HEADS * (QK_NOPE + QK_ROPE)), Q_LORA ** -0.5),
        "mla_kv_norm": gain(ks[6], (N_MLA, KV_LORA)),
        "mla_w_ukv": nrm(ks[7], (N_MLA, KV_LORA, N_MIX_HEADS * (QK_NOPE + V_HEAD)), KV_LORA ** -0.5),
        "gdn_w_in": nrm(ks[8], (N_GDN, D_MODEL, GDN_IN), D_MODEL ** -0.5),
        "gdn_conv": nrm(ks[9], (N_GDN, CONV_WIDTH, 3 * MIX_WIDTH), CONV_WIDTH ** -0.5),
        "gdn_a_log": jnp.log(jax.random.uniform(ks[10], (N_GDN, N_MIX_HEADS), f32, 1.0, 16.0)),
        "gdn_dt_bias": dt + jnp.log(-jnp.expm1(-dt)),
        "gdn_o_norm": gain(ks[12], (N_GDN, HEAD_DIM)),
        "mem_w_kv": nrm(ks[13], (DEPTH, D_MODEL, 2 * MEM_WIDTH), D_MODEL ** -0.5),
        "w_out": nrm(ks[14], (DEPTH, CAT_WIDTH, D_MODEL), CAT_WIDTH ** -0.5 * BETA_INIT),
        "ln1_g": gain(ks[15], (DEPTH, D_MODEL)),
        "ln1_b": nrm(ks[16], (DEPTH, D_MODEL), 0.02),
        "mlp_w1": nrm(ks[17], (DEPTH, D_MODEL, D_FF), D_MODEL ** -0.5),
        "mlp_w2": nrm(ks[18], (DEPTH, D_FF, D_MODEL), D_FF ** -0.5 * BETA_INIT),
        "ln2_g": gain(ks[19], (DEPTH, D_MODEL)),
        "ln2_b": nrm(ks[20], (DEPTH, D_MODEL), 0.02),
    }


def reference(x, mem, positions, mla_w_in, mla_q_norm, mla_w_uq, mla_kv_norm, mla_w_ukv,
              gdn_w_in, gdn_conv, gdn_a_log, gdn_dt_bias, gdn_o_norm, mem_w_kv, w_out,
              ln1_g, ln1_b, mlp_w1, mlp_w2, ln2_g, ln2_b):
    B_, S_, _ = x.shape
    inv_freq = 1.0 / (ROPE_THETA ** (jnp.arange(0, QK_ROPE, 2, dtype=jnp.float32) / QK_ROPE))
    ang = positions.astype(jnp.float32)[:, :, None, None] * inv_freq
    cos = jnp.cos(ang).astype(x.dtype)
    sin = jnp.sin(ang).astype(x.dtype)
    for i in range(DEPTH):
        j = i // 2
        mem_kv = mem @ mem_w_kv[i]
        if i % 2 == 0:
            h = x @ mla_w_in[j]
            mix = mla_mixer(h[..., :MLA_MIX_IN], cos, sin, mla_q_norm[j], mla_w_uq[j],
                            mla_kv_norm[j], mla_w_ukv[j])
        else:
            h = x @ gdn_w_in[j]
            mix = gdn_mixer(h[..., :GDN_MIX_IN], gdn_conv[j], gdn_a_log[j], gdn_dt_bias[j], gdn_o_norm[j])
        q_mem = h[..., -MEM_WIDTH:].reshape(B_, S_, N_MEM_HEADS, HEAD_DIM)
        mem_o = memory_attention(q_mem, mem_kv)
        y = jnp.concatenate([mix, mem_o], -1) @ w_out[i]
        x = layer_norm(ALPHA * x + y, ln1_g[i], ln1_b[i])
        ff = jnp.square(jax.nn.relu(x @ mlp_w1[i])) @ mlp_w2[i]
        x = layer_norm(ALPHA * x + ff, ln2_g[i], ln2_b[i])
    return x
```

```python
import functools

import jax
import jax.numpy as jnp
from jax import lax
from jax.experimental import pallas as pl
from jax.experimental.pallas import tpu as pltpu

F32 = jnp.float32
BF16 = jnp.bfloat16

HEAD_DIM = 128
N_MIX_HEADS = 12
N_MEM_HEADS = 4
MIX_WIDTH = N_MIX_HEADS * HEAD_DIM
MEM_WIDTH = N_MEM_HEADS * HEAD_DIM
Q_LORA = 512
KV_LORA = 512
QK_NOPE = 128
QK_ROPE = 64
ROPE_THETA = 10000.0
CONV_WIDTH = 4
CHUNK = 64
LN_EPS = 1e-5
RMS_EPS = 1e-6
L2_EPS = 1e-6

LANE = 128
SUBLANE = 8
VMEM_LIMIT = 56 * 1024 * 1024

GDN_ROWS = 256
GDN_HEADS = 2


def _params(*sem):
    return pltpu.CompilerParams(dimension_semantics=sem, vmem_limit_bytes=VMEM_LIMIT)


def _dot(a, b):
    return jnp.dot(a, b, preferred_element_type=F32)


def _dot_nt(a, b):
    return lax.dot_general(a, b, (((1,), (1,)), ((), ())), preferred_element_type=F32)


def _dot_tn(a, b):
    return lax.dot_general(a, b, (((0,), (0,)), ((), ())), preferred_element_type=F32)


def _split2(a):
    hi = a.astype(BF16)
    lo = (a - hi.astype(F32)).astype(BF16)
    return hi, lo


def _dot3(a, b):
    ah, al = _split2(a)
    bh, bl = _split2(b)
    return _dot(ah, bh) + (_dot(ah, bl) + _dot(al, bh))


def _layer_norm(r, g, b):
    mu = jnp.mean(r, -1, keepdims=True)
    d = r - mu
    var = jnp.mean(d * d, -1, keepdims=True)
    return d * lax.rsqrt(var + LN_EPS) * g + b


def _rms_norm(x, g):
    return x * lax.rsqrt(jnp.mean(x * x, -1, keepdims=True) + RMS_EPS) * g


def _silu(x):
    return x * jax.nn.sigmoid(x)


def _mm_body(x_ref, w_ref, o_ref):
    o_ref[...] = _dot(x_ref[...].astype(BF16), w_ref[...]).astype(o_ref.dtype)


def _matmul(x, w, out_dtype, tm, tn):
    M, K = x.shape
    N = w.shape[1]
    return pl.pallas_call(
        _mm_body,
        grid=(M // tm, N // tn),
        in_specs=[pl.BlockSpec((tm, K), lambda i, j: (i, 0)),
                  pl.BlockSpec((K, tn), lambda i, j: (0, j))],
        out_specs=pl.BlockSpec((tm, tn), lambda i, j: (i, j)),
        out_shape=jax.ShapeDtypeStruct((M, N), out_dtype),
        compiler_params=_params("parallel", "parallel"),
    )(x, w)


def _outproj_body(mix_ref, mem_ref, wa_ref, wb_ref, x_ref, g_ref, b_ref, o_ref, *, alpha):
    y = _dot(mix_ref[...], wa_ref[...]) + _dot(mem_ref[...], wb_ref[...])
    o_ref[...] = _layer_norm(alpha * x_ref[...] + y, g_ref[...], b_ref[...])


def _outproj_ln(mix, memo, w_mix, w_mem, x, g, b, alpha, tm=256):
    M, D = x.shape
    return pl.pallas_call(
        functools.partial(_outproj_body, alpha=alpha),
        grid=(M // tm,),
        in_specs=[pl.BlockSpec((tm, mix.shape[1]), lambda i: (i, 0)),
                  pl.BlockSpec((tm, memo.shape[1]), lambda i: (i, 0)),
                  pl.BlockSpec(w_mix.shape, lambda i: (0, 0)),
                  pl.BlockSpec(w_mem.shape, lambda i: (0, 0)),
                  pl.BlockSpec((tm, D), lambda i: (i, 0)),
                  pl.BlockSpec((1, D), lambda i: (0, 0)),
                  pl.BlockSpec((1, D), lambda i: (0, 0))],
        out_specs=pl.BlockSpec((tm, D), lambda i: (i, 0)),
        out_shape=jax.ShapeDtypeStruct((M, D), F32),
        compiler_params=_params("parallel"),
    )(mix, memo, w_mix, w_mem, x, g, b)


def _mlp_body(x_ref, w1_ref, w2_ref, g_ref, b_ref, o_ref, xb_ref, acc_ref, *, alpha):
    f = pl.program_id(1)

    @pl.when(f == 0)
    def _():
        xb_ref[...] = x_ref[...].astype(BF16)
        acc_ref[...] = jnp.zeros_like(acc_ref)

    h = jnp.maximum(_dot(xb_ref[...], w1_ref[...]), 0.0)
    acc_ref[...] += _dot((h * h).astype(BF16), w2_ref[...])

    @pl.when(f == pl.num_programs(1) - 1)
    def _():
        o_ref[...] = _layer_norm(alpha * x_ref[...] + acc_ref[...], g_ref[...], b_ref[...])


def _mlp_ln(x, w1, w2, g, b, alpha, tm=512, tf=512):
    M, D = x.shape
    FF = w1.shape[1]
    return pl.pallas_call(
        functools.partial(_mlp_body, alpha=alpha),
        grid=(M // tm, FF // tf),
        in_specs=[pl.BlockSpec((tm, D), lambda i, f: (i, 0)),
                  pl.BlockSpec((D, tf), lambda i, f: (0, f)),
                  pl.BlockSpec((tf, D), lambda i, f: (f, 0)),
                  pl.BlockSpec((1, D), lambda i, f: (0, 0)),
                  pl.BlockSpec((1, D), lambda i, f: (0, 0))],
        out_specs=pl.BlockSpec((tm, D), lambda i, f: (i, 0)),
        out_shape=jax.ShapeDtypeStruct((M, D), F32),
        scratch_shapes=[pltpu.VMEM((tm, D), BF16), pltpu.VMEM((tm, D), F32)],
        compiler_params=_params("parallel", "arbitrary"),
    )(x, w1, w2, g, b)


def _rope_body(pos_ref, freq_ref, cos_ref, sin_ref):
    ang = pos_ref[...].astype(F32) * freq_ref[...]
    lane = lax.broadcasted_iota(jnp.int32, ang.shape, 1)
    keep = lane < QK_ROPE
    cos_ref[...] = jnp.where(keep, jnp.cos(ang), 0.0)
    sin_ref[...] = jnp.where(keep, jnp.sin(ang), 0.0)


def _rope_tables(pos, freq, tm=1024):
    T = pos.shape[0]
    shp = jax.ShapeDtypeStruct((T, LANE), F32)
    return pl.pallas_call(
        _rope_body,
        grid=(T // tm,),
        in_specs=[pl.BlockSpec((tm, 1), lambda i: (i, 0)),
                  pl.BlockSpec((1, LANE), lambda i: (0, 0))],
        out_specs=[pl.BlockSpec((tm, LANE), lambda i: (i, 0))] * 2,
        out_shape=[shp, shp],
        compiler_params=_params("parallel"),
    )(pos, freq)


def _qup_body(c_ref, g_ref, w_ref, cos_ref, sin_ref, o_ref, *, scale):
    cn = _rms_norm(c_ref[...], g_ref[...]).astype(BF16)
    r = _dot(cn, w_ref[...])
    cos = cos_ref[...] * scale
    sin = sin_ref[...] * scale
    hw = 2 * LANE
    rot0 = N_MIX_HEADS * hw
    for h in range(N_MIX_HEADS):
        o_ref[:, h * hw:h * hw + LANE] = (r[:, h * hw:h * hw + LANE] * scale).astype(o_ref.dtype)
        rope = r[:, h * hw + LANE:(h + 1) * hw] * cos + r[:, rot0 + h * LANE:rot0 + (h + 1) * LANE] * sin
        o_ref[:, h * hw + LANE:(h + 1) * hw] = rope.astype(o_ref.dtype)


def _q_up(h, g, w, cosp, sinp, scale, tm=256):
    T = h.shape[0]
    N = N_MIX_HEADS * 2 * LANE
    return pl.pallas_call(
        functools.partial(_qup_body, scale=scale),
        grid=(T // tm,),
        in_specs=[pl.BlockSpec((tm, Q_LORA), lambda i: (i, 0)),
                  pl.BlockSpec((1, Q_LORA), lambda i: (0, 0)),
                  pl.BlockSpec(w.shape, lambda i: (0, 0)),
                  pl.BlockSpec((tm, LANE), lambda i: (i, 0)),
                  pl.BlockSpec((tm, LANE), lambda i: (i, 0))],
        out_specs=pl.BlockSpec((tm, N), lambda i: (i, 0)),
        out_shape=jax.ShapeDtypeStruct((T, N), BF16),
        compiler_params=_params("parallel"),
    )(h, g, w, cosp, sinp)


def _kvup_body(c_ref, kr_ref, g_ref, w_ref, cos_ref, sin_ref, kn_ref, v_ref, krp_ref):
    cn = _rms_norm(c_ref[...], g_ref[...]).astype(BF16)
    r = _dot(cn, w_ref[...])
    kn_ref[...] = r[:, :MIX_WIDTH].astype(kn_ref.dtype)
    v_ref[...] = r[:, MIX_WIDTH:].astype(v_ref.dtype)
    kr = kr_ref[...]
    swapped = pltpu.roll(kr, QK_ROPE, 1)
    krp_ref[...] = (kr * cos_ref[...] + swapped * sin_ref[...]).astype(krp_ref.dtype)


def _kv_up(h, g, w, cosp, sinp, tm=256):
    T = h.shape[0]
    kr_blk = (Q_LORA + KV_LORA + MEM_WIDTH) // LANE
    return pl.pallas_call(
        _kvup_body,
        grid=(T // tm,),
        in_specs=[pl.BlockSpec((tm, KV_LORA), lambda i: (i, 1)),
                  pl.BlockSpec((tm, LANE), lambda i: (i, kr_blk)),
                  pl.BlockSpec((1, KV_LORA), lambda i: (0, 0)),
                  pl.BlockSpec(w.shape, lambda i: (0, 0)),
                  pl.BlockSpec((tm, LANE), lambda i: (i, 0)),
                  pl.BlockSpec((tm, LANE), lambda i: (i, 0))],
        out_specs=[pl.BlockSpec((tm, MIX_WIDTH), lambda i: (i, 0)),
                   pl.BlockSpec((tm, MIX_WIDTH), lambda i: (i, 0)),
                   pl.BlockSpec((tm, LANE), lambda i: (i, 0))],
        out_shape=[jax.ShapeDtypeStruct((T, MIX_WIDTH), BF16),
                   jax.ShapeDtypeStruct((T, MIX_WIDTH), BF16),
                   jax.ShapeDtypeStruct((T, LANE), BF16)],
        compiler_params=_params("parallel"),
    )(h, h, g, w, cosp, sinp)


def _flash_body(q_ref, kn_ref, kr_ref, v_ref, o_ref, *, tq):
    qi = pl.program_id(2)
    q = q_ref[0]

    def block(j, masked, carry):
        m, l, acc = carry
        start = pl.multiple_of(j * tq, tq)
        k = jnp.concatenate([kn_ref[0, pl.ds(start, tq), :], kr_ref[0, pl.ds(start, tq), :]], axis=1)
        s = _dot_nt(q, k)
        if masked:
            row = lax.broadcasted_iota(jnp.int32, s.shape, 0)
            col = lax.broadcasted_iota(jnp.int32, s.shape, 1)
            s = jnp.where(col <= row, s, -jnp.inf)
        m_new = jnp.maximum(m, jnp.max(s, -1, keepdims=True))
        a = jnp.exp(m - m_new)
        p = jnp.exp(s - m_new)
        l = a * l + jnp.sum(p, -1, keepdims=True)
        acc = a * acc + _dot(p.astype(BF16), v_ref[0, pl.ds(start, tq), :])
        return m_new, l, acc

    init = (jnp.full((tq, 1), -jnp.inf, F32), jnp.zeros((tq, 1), F32), jnp.zeros((tq, HEAD_DIM), F32))
    carry = lax.fori_loop(0, qi, lambda j, c: block(j, False, c), init)
    _, l, acc = block(qi, True, carry)
    o_ref[0] = (acc / l).astype(o_ref.dtype)


def _flash(q, kn, kr, v, tq=256):
    B, S, _ = q.shape
    return pl.pallas_call(
        functools.partial(_flash_body, tq=tq),
        grid=(B, N_MIX_HEADS, S // tq),
        in_specs=[pl.BlockSpec((1, tq, 2 * LANE), lambda b, h, i: (b, i, h)),
                  pl.BlockSpec((1, S, HEAD_DIM), lambda b, h, i: (b, 0, h)),
                  pl.BlockSpec((1, S, LANE), lambda b, h, i: (b, 0, 0)),
                  pl.BlockSpec((1, S, HEAD_DIM), lambda b, h, i: (b, 0, h))],
        out_specs=pl.BlockSpec((1, tq, HEAD_DIM), lambda b, h, i: (b, i, h)),
        out_shape=jax.ShapeDtypeStruct((B, S, MIX_WIDTH), BF16),
        compiler_params=_params("parallel", "parallel", "arbitrary"),
    )(q, kn, kr, v)


def _memattn_body(q_ref, kv_ref, o_ref, *, scale):
    for h in range(N_MEM_HEADS):
        q = (q_ref[0, :, h * HEAD_DIM:(h + 1) * HEAD_DIM] * scale).astype(BF16)
        k = kv_ref[0, :, h * HEAD_DIM:(h + 1) * HEAD_DIM].astype(BF16)
        v = kv_ref[0, :, MEM_WIDTH + h * HEAD_DIM:MEM_WIDTH + (h + 1) * HEAD_DIM].astype(BF16)
        s = _dot_nt(q, k)
        p = jnp.exp(s - jnp.max(s, -1, keepdims=True))
        o = _dot(p.astype(BF16), v) / jnp.sum(p, -1, keepdims=True)
        o_ref[0, :, h * HEAD_DIM:(h + 1) * HEAD_DIM] = o.astype(o_ref.dtype)


def _mem_attention(h3, col_blk, mem_kv, tm=512):
    B, S, _ = h3.shape
    M = mem_kv.shape[1]
    return pl.pallas_call(
        functools.partial(_memattn_body, scale=HEAD_DIM ** -0.5),
        grid=(B, S // tm),
        in_specs=[pl.BlockSpec((1, tm, MEM_WIDTH), lambda b, i: (b, i, col_blk)),
                  pl.BlockSpec((1, M, 2 * MEM_WIDTH), lambda b, i: (b, 0, 0))],
        out_specs=pl.BlockSpec((1, tm, MEM_WIDTH), lambda b, i: (b, i, 0)),
        out_shape=jax.ShapeDtypeStruct((B, S, MEM_WIDTH), BF16),
        compiler_params=_params("parallel", "parallel"),
    )(h3, mem_kv)


def _gate_body(ab_ref, alog_ref, dt_ref, g_ref, beta_ref):
    ab = ab_ref[...]
    g_ref[...] = -jnp.exp(alog_ref[...]) * jax.nn.softplus(ab + dt_ref[...])
    beta_ref[...] = jax.nn.sigmoid(ab)


def _gdn_gates(h, col_blk, alog, dt, tm=1024):
    T = h.shape[0]
    shp = jax.ShapeDtypeStruct((T, LANE), F32)
    return pl.pallas_call(
        _gate_body,
        grid=(T // tm,),
        in_specs=[pl.BlockSpec((tm, LANE), lambda i: (i, col_blk)),
                  pl.BlockSpec((1, LANE), lambda i: (0, 0)),
                  pl.BlockSpec((1, LANE), lambda i: (0, 0))],
        out_specs=[pl.BlockSpec((tm, LANE), lambda i: (i, 0))] * 2,
        out_shape=[shp, shp],
        compiler_params=_params("parallel"),
    )(h, alog, dt)


def _unit_lower_inverse(L, row, col):
    same16 = (row >> 4) == (col >> 4)
    same32 = (row >> 5) == (col >> 5)
    eye = jnp.where(row == col, 1.0, 0.0).astype(F32)
    n1 = jnp.where(same16, -L, 0.0)
    p = eye + n1
    n2 = _dot3(n1, n1)
    p = p + _dot3(p, n2)
    n4 = _dot3(n2, n2)
    p = p + _dot3(p, n4)
    n8 = _dot3(n4, n4)
    p = p + _dot3(p, n8)
    c1 = jnp.where(jnp.logical_and(same32, jnp.logical_not(same16)), L, 0.0)
    t1 = p - _dot3(_dot3(p, c1), p)
    c2 = jnp.where(same32, 0.0, L)
    return t1 - _dot3(_dot3(t1, c2), t1)


def _causal_conv_silu(cur, prev, w):
    R = cur.shape[0]
    xe = jnp.concatenate([prev, cur], axis=0)
    y = xe[SUBLANE:] * w[CONV_WIDTH - 1:CONV_WIDTH, :]
    for d in range(1, CONV_WIDTH):
        y = y + pltpu.roll(xe, d, 0)[SUBLANE:] * w[CONV_WIDTH - 1 - d:CONV_WIDTH - d, :]
    return _silu(y[:R])


def _l2_normalize(x):
    return x * lax.rsqrt(jnp.sum(x * x, -1, keepdims=True) + L2_EPS)


def _gdn_body(q_ref, k_ref, v_ref, z_ref, qp_ref, kp_ref, vp_ref, wq_ref, wk_ref, wv_ref,
              gcol_ref, grow_ref, onorm_ref, o_ref, state_ref):
    R = q_ref.shape[1]
    step = pl.program_id(2)

    @pl.when(step == 0)
    def _():
        state_ref[...] = jnp.zeros_like(state_ref)

    not_first = (step > 0).astype(F32)
    row = lax.broadcasted_iota(jnp.int32, (R, R), 0)
    col = lax.broadcasted_iota(jnp.int32, (R, R), 1)
    same_chunk = (row // CHUNK) == (col // CHUNK)
    tril = jnp.logical_and(same_chunk, col <= row)
    strict = jnp.logical_and(same_chunk, col < row)
    m_tril = jnp.where(tril, 1.0, 0.0).astype(BF16)
    m_full = jnp.where(same_chunk, 1.0, 0.0).astype(BF16)
    m_triu = jnp.where(jnp.logical_and(same_chunk, row <= col), 1.0, 0.0).astype(BF16)

    def split3(a):
        a1 = a.astype(BF16)
        r1 = a - a1.astype(F32)
        a2 = r1.astype(BF16)
        a3 = (r1 - a2.astype(F32)).astype(BF16)
        return a1, a2, a3

    for hh in range(GDN_HEADS):
        ln = slice(hh * HEAD_DIM, (hh + 1) * HEAD_DIM)
        xq = _causal_conv_silu(q_ref[0, :, ln], qp_ref[0, :, ln] * not_first, wq_ref[:, ln])
        xk = _causal_conv_silu(k_ref[0, :, ln], kp_ref[0, :, ln] * not_first, wk_ref[:, ln])
        v = _causal_conv_silu(v_ref[0, :, ln], vp_ref[0, :, ln] * not_first, wv_ref[:, ln])
        q = _l2_normalize(xq) * (HEAD_DIM ** -0.5)
        k = _l2_normalize(xk)

        g_c = gcol_ref[0, :, hh:hh + 1]
        beta_c = gcol_ref[0, :, GDN_HEADS + hh:GDN_HEADS + hh + 1]
        g_r = grow_ref[0, 0, hh:hh + 1, :]

        gparts = split3(jnp.broadcast_to(g_c, (R, HEAD_DIM)))
        gc = _dot(m_tril, gparts[0]) + _dot(m_tril, gparts[1]) + _dot(m_tril, gparts[2])
        g_end = _dot(m_full, gparts[0]) + _dot(m_full, gparts[1]) + _dot(m_full, gparts[2])
        rparts = split3(jnp.broadcast_to(g_r, (SUBLANE, R)))
        gc_row = _dot(rparts[0], m_triu) + _dot(rparts[1], m_triu) + _dot(rparts[2], m_triu)

        diff = jnp.concatenate([gc] * (R // HEAD_DIM), axis=1) - gc_row[0:1, :]
        decay = jnp.where(tril, jnp.exp(jnp.where(tril, diff, 0.0)), 0.0)

        kb = k * beta_c
        kbf = k.astype(BF16)
        lmat = jnp.where(strict, _dot_nt(kb.astype(BF16), kbf) * decay, 0.0)
        tinv = _unit_lower_inverse(lmat, row, col)
        egc = jnp.exp(gc)
        sol = _dot3(tinv, jnp.concatenate([v * beta_c, kb * egc], axis=1))
        u = sol[:, :HEAD_DIM]
        w = sol[:, HEAD_DIM:].astype(BF16)

        a_qk = jnp.where(tril, _dot_nt(q.astype(BF16), kbf) * decay, 0.0)
        q_dec = (q * egc).astype(BF16)
        k_dec = (k * jnp.exp(g_end - gc)).astype(BF16)
        g_last = jnp.exp(g_end)

        state = state_ref[hh]
        v_new, o_inter = [], []
        for c in range(R // CHUNK):
            rows = slice(c * CHUNK, (c + 1) * CHUNK)
            sb = state.astype(BF16)
            vn = u[rows] - _dot(w[rows], sb)
            o_inter.append(_dot(q_dec[rows], sb))
            state = state * g_last[c * CHUNK:c * CHUNK + 1, :] + _dot_tn(k_dec[rows], vn.astype(BF16))
            v_new.append(vn)
        state_ref[hh] = state

        o = jnp.concatenate(o_inter, axis=0) + _dot(a_qk.astype(BF16), jnp.concatenate(v_new, axis=0).astype(BF16))
        o = _rms_norm(o, onorm_ref[...]) * _silu(z_ref[0, :, ln])
        o_ref[0, :, ln] = o.astype(o_ref.dtype)


def _gdn_mix(h3, conv_w, gcol, grow, o_norm):
    B, S, _ = h3.shape
    R, G = GDN_ROWS, GDN_HEADS
    W = G * HEAD_DIM
    ng = N_MIX_HEADS // G
    rb = R // SUBLANE

    def cur(off):
        return pl.BlockSpec((1, R, W), lambda b, g, i: (b, i, off * ng + g))

    def prev(off):
        return pl.BlockSpec((1, SUBLANE, W), lambda b, g, i: (b, jnp.maximum(i * rb - 1, 0), off * ng + g))

    def convw(off):
        return pl.BlockSpec((CONV_WIDTH, W), lambda b, g, i: (0, off * ng + g))

    return pl.pallas_call(
        _gdn_body,
        grid=(B, ng, S // R),
        in_specs=[cur(0), cur(1), cur(2), cur(3), prev(0), prev(1), prev(2),
                  convw(0), convw(1), convw(2),
                  pl.BlockSpec((1, R, LANE), lambda b, g, i: (b, i, g)),
                  pl.BlockSpec((1, 1, SUBLANE, R), lambda b, g, i: (b, g, 0, i)),
                  pl.BlockSpec((1, HEAD_DIM), lambda b, g, i: (0, 0))],
        out_specs=pl.BlockSpec((1, R, W), lambda b, g, i: (b, i, g)),
        out_shape=jax.ShapeDtypeStruct((B, S, MIX_WIDTH), BF16),
        scratch_shapes=[pltpu.VMEM((G, HEAD_DIM, HEAD_DIM), F32)],
        compiler_params=_params("parallel", "parallel", "arbitrary"),
    )(h3, h3, h3, h3, h3, h3, h3, conv_w, conv_w, conv_w, gcol, grow, o_norm)


def _rotate_half_cols(w):
    half = w.shape[-1] // 2
    return jnp.concatenate([-w[..., half:], w[..., :half]], -1)


def _mla_weights(w_in, w_uq, w_ukv):
    D = w_in.shape[0]
    c = w_in[:, :Q_LORA + KV_LORA]
    k_r = w_in[:, Q_LORA + KV_LORA:Q_LORA + KV_LORA + QK_ROPE]
    q_mem = w_in[:, Q_LORA + KV_LORA + QK_ROPE:]
    w_in_p = jnp.concatenate([c, q_mem, k_r, _rotate_half_cols(k_r)], 1).astype(BF16)

    uq = w_uq.reshape(Q_LORA, N_MIX_HEADS, QK_NOPE + QK_ROPE)
    nope, rope = uq[..., :QK_NOPE], uq[..., QK_NOPE:]
    zeros = jnp.zeros_like(rope)
    main = jnp.concatenate([nope, rope, zeros], -1).reshape(Q_LORA, -1)
    rot = jnp.concatenate([_rotate_half_cols(rope), zeros], -1).reshape(Q_LORA, -1)
    w_uq_p = jnp.concatenate([main, rot], 1).astype(BF16)

    ukv = w_ukv.reshape(KV_LORA, N_MIX_HEADS, QK_NOPE + HEAD_DIM)
    w_ukv_p = jnp.concatenate([ukv[..., :QK_NOPE].reshape(KV_LORA, -1),
                               ukv[..., QK_NOPE:].reshape(KV_LORA, -1)], 1).astype(BF16)
    del D
    return w_in_p, w_uq_p, w_ukv_p


def _gdn_in_weights(w_in):
    D = w_in.shape[0]
    mix4 = w_in[:, :4 * MIX_WIDTH]
    ab = w_in[:, 4 * MIX_WIDTH:4 * MIX_WIDTH + 2 * N_MIX_HEADS]
    q_mem = w_in[:, 4 * MIX_WIDTH + 2 * N_MIX_HEADS:]
    pad = jnp.zeros((D, 2 * LANE - 2 * N_MIX_HEADS), w_in.dtype)
    return jnp.concatenate([mix4, q_mem, ab, pad], 1).astype(BF16)


def _pad_lanes(v):
    return jnp.pad(v, (0, LANE - v.shape[0]))[None, :]


def kernel(x, mem, positions, mla_w_in, mla_q_norm, mla_w_uq, mla_kv_norm, mla_w_ukv, gdn_w_in, gdn_conv, gdn_a_log, gdn_dt_bias, gdn_o_norm, mem_w_kv, w_out, ln1_g, ln1_b, mlp_w1, mlp_w2, ln2_g, ln2_b):
    B, S, D = x.shape
    T = B * S
    depth = w_out.shape[0]
    alpha = (2 * depth) ** 0.25
    G = GDN_HEADS
    ng = N_MIX_HEADS // G

    inv_freq = 1.0 / (ROPE_THETA ** (jnp.arange(0, QK_ROPE, 2, dtype=F32) / QK_ROPE))
    freq = _pad_lanes(jnp.concatenate([inv_freq, inv_freq]))
    cosp, sinp = _rope_tables(positions.reshape(T, 1), freq)

    xf = x.reshape(T, D)
    mem2 = mem.reshape(B * mem.shape[1], D)
    for i in range(depth):
        j = i // 2
        mem_kv = _matmul(mem2, mem_w_kv[i].astype(BF16), F32, tm=256, tn=512)
        mem_kv = mem_kv.reshape(B, mem.shape[1], 2 * MEM_WIDTH)
        if i % 2 == 0:
            w_in_p, w_uq_p, w_ukv_p = _mla_weights(mla_w_in[j], mla_w_uq[j], mla_w_ukv[j])
            h = _matmul(xf, w_in_p, F32, tm=512, tn=w_in_p.shape[1])
            q = _q_up(h, mla_q_norm[j][None, :], w_uq_p, cosp, sinp, (QK_NOPE + QK_ROPE) ** -0.5)
            kn, v, krp = _kv_up(h, mla_kv_norm[j][None, :], w_ukv_p, cosp, sinp)
            mix = _flash(q.reshape(B, S, -1), kn.reshape(B, S, -1), krp.reshape(B, S, -1),
                         v.reshape(B, S, -1)).reshape(T, MIX_WIDTH)
            mem_o = _mem_attention(h.reshape(B, S, -1), (Q_LORA + KV_LORA) // MEM_WIDTH, mem_kv)
        else:
            w_in_p = _gdn_in_weights(gdn_w_in[j])
            h = _matmul(xf, w_in_p, F32, tm=512, tn=768)
            ab_blk = (4 * MIX_WIDTH + MEM_WIDTH) // LANE
            alog = _pad_lanes(gdn_a_log[j])
            dt = _pad_lanes(jnp.concatenate([gdn_dt_bias[j]]))
            g_all, beta_all = _gdn_gates(h, ab_blk, alog, dt)
            g = g_all[:, :N_MIX_HEADS].reshape(B, S, ng, G)
            beta = beta_all[:, N_MIX_HEADS:2 * N_MIX_HEADS].reshape(B, S, ng, G)
            gb = jnp.concatenate([g, beta], -1)
            gcol = jnp.pad(gb, ((0, 0), (0, 0), (0, 0), (0, LANE - 2 * G))).reshape(B, S, ng * LANE)
            grow = jnp.pad(jnp.transpose(g, (0, 2, 3, 1)), ((0, 0), (0, 0), (0, SUBLANE - G), (0, 0)))
            h3 = h.reshape(B, S, -1)
            mix = _gdn_mix(h3, gdn_conv[j], gcol, grow, gdn_o_norm[j][None, :]).reshape(T, MIX_WIDTH)
            mem_o = _mem_attention(h3, 4 * MIX_WIDTH // MEM_WIDTH, mem_kv)
        wo = w_out[i].astype(BF16)
        xf = _outproj_ln(mix, mem_o.reshape(T, MEM_WIDTH), wo[:MIX_WIDTH], wo[MIX_WIDTH:], xf,
                         ln1_g[i][None, :], ln1_b[i][None, :], alpha)
        xf = _mlp_ln(xf, mlp_w1[i].astype(BF16), mlp_w2[i].astype(BF16),
                     ln2_g[i][None, :], ln2_b[i][None, :], alpha)
    return xf.reshape(B, S, D)
```

```python
import functools

import jax
import jax.numpy as jnp
from jax import lax
from jax.experimental import pallas as pl
from jax.experimental.pallas import tpu as pltpu

F32 = jnp.float32
BF16 = jnp.bfloat16

HEAD_DIM = 128
N_MIX_HEADS = 12
N_MEM_HEADS = 4
MIX_WIDTH = N_MIX_HEADS * HEAD_DIM
MEM_WIDTH = N_MEM_HEADS * HEAD_DIM
Q_LORA = 512
KV_LORA = 512
QK_NOPE = 128
QK_ROPE = 64
ROPE_THETA = 10000.0
CONV_WIDTH = 4
CHUNK = 64
LN_EPS = 1e-5
RMS_EPS = 1e-6
L2_EPS = 1e-6

LANE = 128
SUBLANE = 8
VMEM_LIMIT = 56 * 1024 * 1024

GDN_ROWS = 256
GDN_HEADS = 4
LOG2_E = 1.4426950408889634
MASKED_LOG = -1e30


def _params(*sem):
    return pltpu.CompilerParams(dimension_semantics=sem, vmem_limit_bytes=VMEM_LIMIT)


def _dot(a, b):
    return jnp.dot(a, b, preferred_element_type=F32)


def _dot_nt(a, b):
    return lax.dot_general(a, b, (((1,), (1,)), ((), ())), preferred_element_type=F32)


def _dot_tn(a, b):
    return lax.dot_general(a, b, (((0,), (0,)), ((), ())), preferred_element_type=F32)


def _layer_norm(r, g, b):
    mu = jnp.mean(r, -1, keepdims=True)
    d = r - mu
    var = jnp.mean(d * d, -1, keepdims=True)
    return d * lax.rsqrt(var + LN_EPS) * g + b


def _rms_norm(x, g):
    return x * lax.rsqrt(jnp.mean(x * x, -1, keepdims=True) + RMS_EPS) * g


def _silu(x):
    return x * jax.nn.sigmoid(x)


def _mm_body(x_ref, w_ref, o_ref):
    o_ref[...] = _dot(x_ref[...].astype(BF16), w_ref[...]).astype(o_ref.dtype)


def _matmul(x, w, out_dtype, tm, tn):
    M, K = x.shape
    N = w.shape[1]
    return pl.pallas_call(
        _mm_body,
        name="matmul",
        grid=(M // tm, N // tn),
        in_specs=[pl.BlockSpec((tm, K), lambda i, j: (i, 0)),
                  pl.BlockSpec((K, tn), lambda i, j: (0, j))],
        out_specs=pl.BlockSpec((tm, tn), lambda i, j: (i, j)),
        out_shape=jax.ShapeDtypeStruct((M, N), out_dtype),
        compiler_params=_params("parallel", "parallel"),
    )(x, w)


def _outproj_body(mix_ref, mem_ref, wa_ref, wb_ref, x_ref, g_ref, b_ref, o_ref, *, alpha):
    y = _dot(mix_ref[...], wa_ref[...]) + _dot(mem_ref[...], wb_ref[...])
    o_ref[...] = _layer_norm(alpha * x_ref[...] + y, g_ref[...], b_ref[...])


def _outproj_ln(mix, memo, w_mix, w_mem, x, g, b, alpha, tm=256):
    M, D = x.shape
    return pl.pallas_call(
        functools.partial(_outproj_body, alpha=alpha),
        name="outproj_ln",
        grid=(M // tm,),
        in_specs=[pl.BlockSpec((tm, mix.shape[1]), lambda i: (i, 0)),
                  pl.BlockSpec((tm, memo.shape[1]), lambda i: (i, 0)),
                  pl.BlockSpec(w_mix.shape, lambda i: (0, 0)),
                  pl.BlockSpec(w_mem.shape, lambda i: (0, 0)),
                  pl.BlockSpec((tm, D), lambda i: (i, 0)),
                  pl.BlockSpec((1, D), lambda i: (0, 0)),
                  pl.BlockSpec((1, D), lambda i: (0, 0))],
        out_specs=pl.BlockSpec((tm, D), lambda i: (i, 0)),
        out_shape=jax.ShapeDtypeStruct((M, D), F32),
        compiler_params=_params("parallel"),
    )(mix, memo, w_mix, w_mem, x, g, b)


def _mlp_body(x_ref, w1_ref, w2_ref, g_ref, b_ref, o_ref, xb_ref, acc_ref, *, alpha):
    f = pl.program_id(1)

    @pl.when(f == 0)
    def _():
        xb_ref[...] = x_ref[...].astype(BF16)
        acc_ref[...] = jnp.zeros_like(acc_ref)

    h = jnp.maximum(_dot(xb_ref[...], w1_ref[...]), 0.0)
    acc_ref[...] += _dot((h * h).astype(BF16), w2_ref[...])

    @pl.when(f == pl.num_programs(1) - 1)
    def _():
        o_ref[...] = _layer_norm(alpha * x_ref[...] + acc_ref[...], g_ref[...], b_ref[...])


def _mlp_ln(x, w1, w2, g, b, alpha, tm=512, tf=1024):
    M, D = x.shape
    FF = w1.shape[1]
    return pl.pallas_call(
        functools.partial(_mlp_body, alpha=alpha),
        name="mlp_ln",
        grid=(M // tm, FF // tf),
        in_specs=[pl.BlockSpec((tm, D), lambda i, f: (i, 0)),
                  pl.BlockSpec((D, tf), lambda i, f: (0, f)),
                  pl.BlockSpec((tf, D), lambda i, f: (f, 0)),
                  pl.BlockSpec((1, D), lambda i, f: (0, 0)),
                  pl.BlockSpec((1, D), lambda i, f: (0, 0))],
        out_specs=pl.BlockSpec((tm, D), lambda i, f: (i, 0)),
        out_shape=jax.ShapeDtypeStruct((M, D), F32),
        scratch_shapes=[pltpu.VMEM((tm, D), BF16), pltpu.VMEM((tm, D), F32)],
        compiler_params=_params("parallel", "arbitrary"),
    )(x, w1, w2, g, b)


def _rope_body(pos_ref, freq_ref, cos_ref, sin_ref):
    ang = pos_ref[...].astype(F32) * freq_ref[...]
    lane = lax.broadcasted_iota(jnp.int32, ang.shape, 1)
    keep = lane < QK_ROPE
    cos_ref[...] = jnp.where(keep, jnp.cos(ang), 0.0)
    sin_ref[...] = jnp.where(keep, jnp.sin(ang), 0.0)


def _rope_tables(pos, freq, tm=1024):
    T = pos.shape[0]
    shp = jax.ShapeDtypeStruct((T, LANE), F32)
    return pl.pallas_call(
        _rope_body,
        name="rope_tables",
        grid=(T // tm,),
        in_specs=[pl.BlockSpec((tm, 1), lambda i: (i, 0)),
                  pl.BlockSpec((1, LANE), lambda i: (0, 0))],
        out_specs=[pl.BlockSpec((tm, LANE), lambda i: (i, 0))] * 2,
        out_shape=[shp, shp],
        compiler_params=_params("parallel"),
    )(pos, freq)


def _qup_body(c_ref, g_ref, w_ref, cos_ref, sin_ref, o_ref, *, scale):
    cn = _rms_norm(c_ref[...], g_ref[...]).astype(BF16)
    r = _dot(cn, w_ref[...])
    cos = cos_ref[...] * scale
    sin = sin_ref[...] * scale
    hw = 2 * LANE
    rot0 = N_MIX_HEADS * hw
    for h in range(N_MIX_HEADS):
        o_ref[:, h * hw:h * hw + LANE] = (r[:, h * hw:h * hw + LANE] * scale).astype(o_ref.dtype)
        rope = r[:, h * hw + LANE:(h + 1) * hw] * cos + r[:, rot0 + h * LANE:rot0 + (h + 1) * LANE] * sin
        o_ref[:, h * hw + LANE:(h + 1) * hw] = rope.astype(o_ref.dtype)


def _q_up(h, g, w, cosp, sinp, scale, tm=256):
    T = h.shape[0]
    N = N_MIX_HEADS * 2 * LANE
    return pl.pallas_call(
        functools.partial(_qup_body, scale=scale),
        name="mla_q_up",
        grid=(T // tm,),
        in_specs=[pl.BlockSpec((tm, Q_LORA), lambda i: (i, 0)),
                  pl.BlockSpec((1, Q_LORA), lambda i: (0, 0)),
                  pl.BlockSpec(w.shape, lambda i: (0, 0)),
                  pl.BlockSpec((tm, LANE), lambda i: (i, 0)),
                  pl.BlockSpec((tm, LANE), lambda i: (i, 0))],
        out_specs=pl.BlockSpec((tm, N), lambda i: (i, 0)),
        out_shape=jax.ShapeDtypeStruct((T, N), BF16),
        compiler_params=_params("parallel"),
    )(h, g, w, cosp, sinp)


def _kvup_body(c_ref, kr_ref, g_ref, w_ref, cos_ref, sin_ref, kn_ref, v_ref, krp_ref):
    cn = _rms_norm(c_ref[...], g_ref[...]).astype(BF16)
    r = _dot(cn, w_ref[...])
    kn_ref[...] = r[:, :MIX_WIDTH].astype(kn_ref.dtype)
    v_ref[...] = r[:, MIX_WIDTH:].astype(v_ref.dtype)
    kr = kr_ref[...]
    swapped = pltpu.roll(kr, QK_ROPE, 1)
    krp_ref[...] = (kr * cos_ref[...] + swapped * sin_ref[...]).astype(krp_ref.dtype)


def _kv_up(h, g, w, cosp, sinp, tm=256):
    T = h.shape[0]
    kr_blk = (Q_LORA + KV_LORA + MEM_WIDTH) // LANE
    return pl.pallas_call(
        _kvup_body,
        name="mla_kv_up",
        grid=(T // tm,),
        in_specs=[pl.BlockSpec((tm, KV_LORA), lambda i: (i, 1)),
                  pl.BlockSpec((tm, LANE), lambda i: (i, kr_blk)),
                  pl.BlockSpec((1, KV_LORA), lambda i: (0, 0)),
                  pl.BlockSpec(w.shape, lambda i: (0, 0)),
                  pl.BlockSpec((tm, LANE), lambda i: (i, 0)),
                  pl.BlockSpec((tm, LANE), lambda i: (i, 0))],
        out_specs=[pl.BlockSpec((tm, MIX_WIDTH), lambda i: (i, 0)),
                   pl.BlockSpec((tm, MIX_WIDTH), lambda i: (i, 0)),
                   pl.BlockSpec((tm, LANE), lambda i: (i, 0))],
        out_shape=[jax.ShapeDtypeStruct((T, MIX_WIDTH), BF16),
                   jax.ShapeDtypeStruct((T, MIX_WIDTH), BF16),
                   jax.ShapeDtypeStruct((T, LANE), BF16)],
        compiler_params=_params("parallel"),
    )(h, h, g, w, cosp, sinp)


def _flash_body(q_ref, kn_ref, kr_ref, v_ref, o_ref, *, tq, nh):
    qi = pl.program_id(2)
    qw = 2 * LANE

    def block(j, masked, carry):
        start = pl.multiple_of(j * tq, tq)
        kr = kr_ref[0, pl.ds(start, tq), :]
        if masked:
            row = lax.broadcasted_iota(jnp.int32, (tq, tq), 0)
            col = lax.broadcasted_iota(jnp.int32, (tq, tq), 1)
            keep = col <= row
        hs = range(nh)
        hd = [slice(h * HEAD_DIM, (h + 1) * HEAD_DIM) for h in hs]
        m, l, acc = carry[0::3], carry[1::3], carry[2::3]
        ks = [jnp.concatenate([kn_ref[0, pl.ds(start, tq), hd[h]], kr], axis=1) for h in hs]
        s = [_dot_nt(q_ref[0, :, h * qw:(h + 1) * qw], ks[h]) for h in hs]
        if masked:
            s = [jnp.where(keep, s[h], -jnp.inf) for h in hs]
        m_new = [jnp.maximum(m[h], jnp.max(s[h], -1, keepdims=True)) for h in hs]
        a = [jnp.exp2(m[h] - m_new[h]) for h in hs]
        p = [jnp.exp2(s[h] - m_new[h]) for h in hs]
        l = [a[h] * l[h] + jnp.sum(p[h], -1, keepdims=True) for h in hs]
        pv = [_dot(p[h].astype(BF16), v_ref[0, pl.ds(start, tq), hd[h]]) for h in hs]
        acc = [a[h] * acc[h] + pv[h] for h in hs]
        out = []
        for h in hs:
            out += [m_new[h], l[h], acc[h]]
        return tuple(out)

    init = (jnp.full((tq, 1), -jnp.inf, F32), jnp.zeros((tq, 1), F32), jnp.zeros((tq, HEAD_DIM), F32)) * nh
    carry = lax.fori_loop(0, qi, lambda j, c: block(j, False, c), init)
    res = block(qi, True, carry)
    o_ref[0] = jnp.concatenate([res[3 * h + 2] / res[3 * h + 1] for h in range(nh)], axis=1).astype(o_ref.dtype)


def _flash(q, kn, kr, v, tq=512, nh=2):
    B, S, _ = q.shape
    return pl.pallas_call(
        functools.partial(_flash_body, tq=tq, nh=nh),
        name="mla_flash",
        grid=(B, N_MIX_HEADS // nh, S // tq),
        in_specs=[pl.BlockSpec((1, tq, nh * 2 * LANE), lambda b, h, i: (b, i, h)),
                  pl.BlockSpec((1, S, nh * HEAD_DIM), lambda b, h, i: (b, 0, h)),
                  pl.BlockSpec((1, S, LANE), lambda b, h, i: (b, 0, 0)),
                  pl.BlockSpec((1, S, nh * HEAD_DIM), lambda b, h, i: (b, 0, h))],
        out_specs=pl.BlockSpec((1, tq, nh * HEAD_DIM), lambda b, h, i: (b, i, h)),
        out_shape=jax.ShapeDtypeStruct((B, S, MIX_WIDTH), BF16),
        compiler_params=_params("parallel", "parallel", "arbitrary"),
    )(q, kn, kr, v)


def _memattn_body(q_ref, kv_ref, o_ref, *, scale):
    for h in range(N_MEM_HEADS):
        q = (q_ref[0, :, h * HEAD_DIM:(h + 1) * HEAD_DIM] * scale).astype(BF16)
        k = kv_ref[0, :, h * HEAD_DIM:(h + 1) * HEAD_DIM].astype(BF16)
        v = kv_ref[0, :, MEM_WIDTH + h * HEAD_DIM:MEM_WIDTH + (h + 1) * HEAD_DIM].astype(BF16)
        s = _dot_nt(q, k)
        p = jnp.exp(s - jnp.max(s, -1, keepdims=True))
        o = _dot(p.astype(BF16), v) / jnp.sum(p, -1, keepdims=True)
        o_ref[0, :, h * HEAD_DIM:(h + 1) * HEAD_DIM] = o.astype(o_ref.dtype)


def _mem_attention(h3, col_blk, mem_kv, tm=512):
    B, S, _ = h3.shape
    M = mem_kv.shape[1]
    return pl.pallas_call(
        functools.partial(_memattn_body, scale=HEAD_DIM ** -0.5),
        name="mem_attention",
        grid=(B, S // tm),
        in_specs=[pl.BlockSpec((1, tm, MEM_WIDTH), lambda b, i: (b, i, col_blk)),
                  pl.BlockSpec((1, M, 2 * MEM_WIDTH), lambda b, i: (b, 0, 0))],
        out_specs=pl.BlockSpec((1, tm, MEM_WIDTH), lambda b, i: (b, i, 0)),
        out_shape=jax.ShapeDtypeStruct((B, S, MEM_WIDTH), BF16),
        compiler_params=_params("parallel", "parallel"),
    )(h3, mem_kv)


def _gate_body(ab_ref, alog_ref, dt_ref, g_ref, beta_ref):
    ab = ab_ref[...]
    g_ref[...] = -jnp.exp(alog_ref[...]) * jax.nn.softplus(ab + dt_ref[...])
    beta_ref[...] = jax.nn.sigmoid(ab)


def _gdn_gates(h, col_blk, alog, dt, tm=1024):
    T = h.shape[0]
    shp = jax.ShapeDtypeStruct((T, LANE), F32)
    return pl.pallas_call(
        _gate_body,
        name="gdn_gates",
        grid=(T // tm,),
        in_specs=[pl.BlockSpec((tm, LANE), lambda i: (i, col_blk)),
                  pl.BlockSpec((1, LANE), lambda i: (0, 0)),
                  pl.BlockSpec((1, LANE), lambda i: (0, 0))],
        out_specs=[pl.BlockSpec((tm, LANE), lambda i: (i, 0))] * 2,
        out_shape=[shp, shp],
        compiler_params=_params("parallel"),
    )(h, alog, dt)


def _unit_lower_inverses(Ls, masks):
    eye, blk16, blk32_only, blk64_only = masks
    idx = range(len(Ls))

    def bf(xs):
        return [x.astype(BF16) for x in xs]

    l16 = [L * blk16 for L in Ls]
    nb = bf([-x for x in l16])
    p = [eye - x for x in l16]
    for _ in range(3):
        nb = bf([_dot(nb[i], nb[i]) for i in idx])
        pb = bf(p)
        p = [p[i] + _dot(pb[i], nb[i]) for i in idx]
    for level in (blk32_only, blk64_only):
        pb = bf(p)
        cb = bf([L * level for L in Ls])
        t = bf([_dot(pb[i], cb[i]) for i in idx])
        p = [p[i] - _dot(t[i], pb[i]) for i in idx]
    return p


def _causal_conv_silu(cur, prev, w):
    R = cur.shape[0]
    xe = jnp.concatenate([prev, cur], axis=0)
    y = xe[SUBLANE:] * w[CONV_WIDTH - 1:CONV_WIDTH, :]
    for d in range(1, CONV_WIDTH):
        y = y + pltpu.roll(xe, d, 0)[SUBLANE:] * w[CONV_WIDTH - 1 - d:CONV_WIDTH - d, :]
    return _silu(y[:R])


def _l2_normalize(x):
    return x * lax.rsqrt(jnp.sum(x * x, -1, keepdims=True) + L2_EPS)


def _gdn_body(q_ref, k_ref, v_ref, z_ref, qp_ref, kp_ref, vp_ref, wq_ref, wk_ref, wv_ref,
              gate_ref, onorm_ref, o_ref, state_ref):
    R = q_ref.shape[1]
    nc = R // CHUNK
    step = pl.program_id(2)

    @pl.when(step == 0)
    def _():
        state_ref[...] = jnp.zeros_like(state_ref)

    not_first = (step > 0).astype(F32)
    row = lax.broadcasted_iota(jnp.int32, (R, R), 0)
    col = lax.broadcasted_iota(jnp.int32, (R, R), 1)

    def same(bits):
        return (row >> bits) == (col >> bits)

    def sel(cond):
        return jnp.where(cond, 1.0, 0.0).astype(F32)

    tril = jnp.logical_and(same(6), col <= row)
    diag = row == col
    m_tril = sel(tril).astype(BF16)
    blk16, blk32, blk64 = sel(same(4)), sel(same(5)), sel(same(6))
    masks = (sel(diag), blk16, blk32 - blk16, blk64 - blk32)

    def split3(a):
        a1 = a.astype(BF16)
        r1 = a - a1.astype(F32)
        a2 = r1.astype(BF16)
        a3 = (r1 - a2.astype(F32)).astype(BF16)
        return a1, a2, a3

    def lane_bcast(a, lane):
        return jnp.broadcast_to(a[:, lane:lane + 1], (R, HEAD_DIM))

    gates = gate_ref[0]
    gparts = split3(gates)
    gc_all = _dot(m_tril, gparts[0]) + _dot(m_tril, gparts[1]) + _dot(m_tril, gparts[2])
    gc_rows = gc_all.T
    g_end_all = jnp.concatenate(
        [jnp.broadcast_to(gc_all[(c + 1) * CHUNK - 1:(c + 1) * CHUNK, :], (CHUNK, LANE)) for c in range(nc)], axis=0)
    egc_all = jnp.exp(gc_all)
    kdec_all = jnp.exp(g_end_all - gc_all)
    glast_all = jnp.exp(g_end_all)

    heads = range(GDN_HEADS)
    lns = [slice(hh * HEAD_DIM, (hh + 1) * HEAD_DIM) for hh in heads]
    q, k, v, beta, egc, decay, kb, kbf, lmat = [], [], [], [], [], [], [], [], []
    for hh, ln in zip(heads, lns):
        xq = _causal_conv_silu(q_ref[0, :, ln], qp_ref[0, :, ln] * not_first, wq_ref[:, ln])
        xk = _causal_conv_silu(k_ref[0, :, ln], kp_ref[0, :, ln] * not_first, wk_ref[:, ln])
        v.append(_causal_conv_silu(v_ref[0, :, ln], vp_ref[0, :, ln] * not_first, wv_ref[:, ln]))
        q.append(_l2_normalize(xq) * (HEAD_DIM ** -0.5))
        k.append(_l2_normalize(xk))
        gc = lane_bcast(gc_all, hh)
        beta.append(lane_bcast(gates, GDN_HEADS + hh))
        egc.append(lane_bcast(egc_all, hh))
        diff = jnp.concatenate([gc] * (R // HEAD_DIM), axis=1) - gc_rows[hh:hh + 1, :]
        decay.append(jnp.exp(jnp.where(tril, diff, MASKED_LOG)))
        kb.append(k[hh] * beta[hh])
        kbf.append(k[hh].astype(BF16))
        lmat.append(jnp.where(diag, 0.0, _dot_nt(kb[hh].astype(BF16), kbf[hh]) * decay[hh]))

    tinv = _unit_lower_inverses(lmat, masks)

    u, w, a_qk, q_dec, k_dec = [], [], [], [], []
    for hh in heads:
        rhs = jnp.concatenate([v[hh] * beta[hh], kb[hh] * egc[hh]], axis=1).astype(BF16)
        sol = _dot(tinv[hh].astype(BF16), rhs)
        u.append(sol[:, :HEAD_DIM])
        w.append(sol[:, HEAD_DIM:].astype(BF16))
        a_qk.append((_dot_nt(q[hh].astype(BF16), kbf[hh]) * decay[hh]).astype(BF16))
        q_dec.append((q[hh] * egc[hh]).astype(BF16))
        k_dec.append((k[hh] * lane_bcast(kdec_all, hh)).astype(BF16))

    state = [state_ref[hh] for hh in heads]
    v_new = [[] for _ in heads]
    o_inter = [[] for _ in heads]
    for c in range(nc):
        rows = slice(c * CHUNK, (c + 1) * CHUNK)
        for hh in heads:
            both = _dot(jnp.concatenate([w[hh][rows], q_dec[hh][rows]], axis=0), state[hh].astype(BF16))
            vn = u[hh][rows] - both[:CHUNK]
            o_inter[hh].append(both[CHUNK:])
            g_last = glast_all[c * CHUNK:c * CHUNK + 1, hh:hh + 1]
            state[hh] = state[hh] * g_last + _dot_tn(k_dec[hh][rows], vn.astype(BF16))
            v_new[hh].append(vn)
    state_ref[...] = jnp.stack(state)

    outs = []
    for hh, ln in zip(heads, lns):
        vn_all = jnp.concatenate(v_new[hh], axis=0).astype(BF16)
        o = jnp.concatenate(o_inter[hh], axis=0) + _dot(a_qk[hh], vn_all)
        outs.append(_rms_norm(o, onorm_ref[...]) * _silu(z_ref[0, :, ln]))
    o_ref[0] = jnp.concatenate(outs, axis=1).astype(o_ref.dtype)


def _gdn_mix(h3, conv_w, gates, o_norm):
    B, S, _ = h3.shape
    R, G = GDN_ROWS, GDN_HEADS
    W = G * HEAD_DIM
    ng = N_MIX_HEADS // G
    rb = R // SUBLANE

    def cur(off):
        return pl.BlockSpec((1, R, W), lambda b, g, i: (b, i, off * ng + g))

    def prev(off):
        return pl.BlockSpec((1, SUBLANE, W), lambda b, g, i: (b, jnp.maximum(i * rb - 1, 0), off * ng + g))

    def convw(off):
        return pl.BlockSpec((CONV_WIDTH, W), lambda b, g, i: (0, off * ng + g))

    return pl.pallas_call(
        _gdn_body,
        name="gdn_delta_rule",
        grid=(B, ng, S // R),
        in_specs=[cur(0), cur(1), cur(2), cur(3), prev(0), prev(1), prev(2),
                  convw(0), convw(1), convw(2),
                  pl.BlockSpec((1, R, LANE), lambda b, g, i: (b, i, g)),
                  pl.BlockSpec((1, HEAD_DIM), lambda b, g, i: (0, 0))],
        out_specs=pl.BlockSpec((1, R, W), lambda b, g, i: (b, i, g)),
        out_shape=jax.ShapeDtypeStruct((B, S, MIX_WIDTH), BF16),
        scratch_shapes=[pltpu.VMEM((G, HEAD_DIM, HEAD_DIM), F32)],
        compiler_params=_params("parallel", "parallel", "arbitrary"),
    )(h3, h3, h3, h3, h3, h3, h3, conv_w, conv_w, conv_w, gates, o_norm)


def _rotate_half_cols(w):
    half = w.shape[-1] // 2
    return jnp.concatenate([-w[..., half:], w[..., :half]], -1)


def _mla_weights(w_in, w_uq, w_ukv):
    D = w_in.shape[0]
    c = w_in[:, :Q_LORA + KV_LORA]
    k_r = w_in[:, Q_LORA + KV_LORA:Q_LORA + KV_LORA + QK_ROPE]
    q_mem = w_in[:, Q_LORA + KV_LORA + QK_ROPE:]
    w_in_p = jnp.concatenate([c, q_mem, k_r, _rotate_half_cols(k_r)], 1).astype(BF16)

    uq = w_uq.reshape(Q_LORA, N_MIX_HEADS, QK_NOPE + QK_ROPE)
    nope, rope = uq[..., :QK_NOPE], uq[..., QK_NOPE:]
    zeros = jnp.zeros_like(rope)
    main = jnp.concatenate([nope, rope, zeros], -1).reshape(Q_LORA, -1)
    rot = jnp.concatenate([_rotate_half_cols(rope), zeros], -1).reshape(Q_LORA, -1)
    w_uq_p = jnp.concatenate([main, rot], 1).astype(BF16)

    ukv = w_ukv.reshape(KV_LORA, N_MIX_HEADS, QK_NOPE + HEAD_DIM)
    w_ukv_p = jnp.concatenate([ukv[..., :QK_NOPE].reshape(KV_LORA, -1),
                               ukv[..., QK_NOPE:].reshape(KV_LORA, -1)], 1).astype(BF16)
    del D
    return w_in_p, w_uq_p, w_ukv_p


def _gdn_in_weights(w_in):
    D = w_in.shape[0]
    mix4 = w_in[:, :4 * MIX_WIDTH]
    ab = w_in[:, 4 * MIX_WIDTH:4 * MIX_WIDTH + 2 * N_MIX_HEADS]
    q_mem = w_in[:, 4 * MIX_WIDTH + 2 * N_MIX_HEADS:]
    pad = jnp.zeros((D, 2 * LANE - 2 * N_MIX_HEADS), w_in.dtype)
    return jnp.concatenate([mix4, q_mem, ab, pad], 1).astype(BF16)


def _pad_lanes(v):
    return jnp.pad(v, (0, LANE - v.shape[0]))[None, :]


def kernel(x, mem, positions, mla_w_in, mla_q_norm, mla_w_uq, mla_kv_norm, mla_w_ukv, gdn_w_in, gdn_conv, gdn_a_log, gdn_dt_bias, gdn_o_norm, mem_w_kv, w_out, ln1_g, ln1_b, mlp_w1, mlp_w2, ln2_g, ln2_b):
    B, S, D = x.shape
    T = B * S
    depth = w_out.shape[0]
    alpha = (2 * depth) ** 0.25
    G = GDN_HEADS
    ng = N_MIX_HEADS // G

    inv_freq = 1.0 / (ROPE_THETA ** (jnp.arange(0, QK_ROPE, 2, dtype=F32) / QK_ROPE))
    freq = _pad_lanes(jnp.concatenate([inv_freq, inv_freq]))
    cosp, sinp = _rope_tables(positions.reshape(T, 1), freq)

    xf = x.reshape(T, D)
    mem2 = mem.reshape(B * mem.shape[1], D)
    for i in range(depth):
        j = i // 2
        mem_kv = _matmul(mem2, mem_w_kv[i].astype(BF16), F32, tm=256, tn=512)
        mem_kv = mem_kv.reshape(B, mem.shape[1], 2 * MEM_WIDTH)
        if i % 2 == 0:
            w_in_p, w_uq_p, w_ukv_p = _mla_weights(mla_w_in[j], mla_w_uq[j], mla_w_ukv[j])
            h = _matmul(xf, w_in_p, F32, tm=512, tn=w_in_p.shape[1])
            q = _q_up(h, mla_q_norm[j][None, :], w_uq_p, cosp, sinp, (QK_NOPE + QK_ROPE) ** -0.5 * LOG2_E)
            kn, v, krp = _kv_up(h, mla_kv_norm[j][None, :], w_ukv_p, cosp, sinp)
            mix = _flash(q.reshape(B, S, -1), kn.reshape(B, S, -1), krp.reshape(B, S, -1),
                         v.reshape(B, S, -1)).reshape(T, MIX_WIDTH)
            mem_o = _mem_attention(h.reshape(B, S, -1), (Q_LORA + KV_LORA) // MEM_WIDTH, mem_kv)
        else:
            w_in_p = _gdn_in_weights(gdn_w_in[j])
            h = _matmul(xf, w_in_p, F32, tm=512, tn=768)
            ab_blk = (4 * MIX_WIDTH + MEM_WIDTH) // LANE
            alog = _pad_lanes(gdn_a_log[j])
            dt = _pad_lanes(gdn_dt_bias[j])
            g_all, beta_all = _gdn_gates(h, ab_blk, alog, dt)
            g = g_all[:, :N_MIX_HEADS].reshape(B, S, ng, G)
            beta = beta_all[:, N_MIX_HEADS:2 * N_MIX_HEADS].reshape(B, S, ng, G)
            gb = jnp.concatenate([g, beta], -1)
            gates = jnp.pad(gb, ((0, 0), (0, 0), (0, 0), (0, LANE - 2 * G))).reshape(B, S, ng * LANE)
            h3 = h.reshape(B, S, -1)
            mix = _gdn_mix(h3, gdn_conv[j], gates, gdn_o_norm[j][None, :]).reshape(T, MIX_WIDTH)
            mem_o = _mem_attention(h3, 4 * MIX_WIDTH // MEM_WIDTH, mem_kv)
        wo = w_out[i].astype(BF16)
        xf = _outproj_ln(mix, mem_o.reshape(T, MEM_WIDTH), wo[:MIX_WIDTH], wo[MIX_WIDTH:], xf,
                         ln1_g[i][None, :], ln1_b[i][None, :], alpha)
        xf = _mlp_ln(xf, mlp_w1[i].astype(BF16), mlp_w2[i].astype(BF16),
                     ln2_g[i][None, :], ln2_b[i][None, :], alpha)
    return xf.reshape(B, S, D)
```

```python
import functools

import jax
import jax.numpy as jnp
from jax import lax
from jax.experimental import pallas as pl
from jax.experimental.pallas import tpu as pltpu

F32 = jnp.float32
BF16 = jnp.bfloat16

HEAD_DIM = 128
N_MIX_HEADS = 12
N_MEM_HEADS = 4
MIX_WIDTH = N_MIX_HEADS * HEAD_DIM
MEM_WIDTH = N_MEM_HEADS * HEAD_DIM
Q_LORA = 512
KV_LORA = 512
QK_NOPE = 128
QK_ROPE = 64
ROPE_THETA = 10000.0
CONV_WIDTH = 4
CHUNK = 64
LN_EPS = 1e-5
RMS_EPS = 1e-6
L2_EPS = 1e-6

LANE = 128
SUBLANE = 8
VMEM_LIMIT = 56 * 1024 * 1024

GDN_ROWS = 256
GDN_HEADS = 4
LOG2_E = 1.4426950408889634
MASKED_LOG = -1e30


def _params(*sem):
    return pltpu.CompilerParams(dimension_semantics=sem, vmem_limit_bytes=VMEM_LIMIT)


def _dot(a, b):
    return jnp.dot(a, b, preferred_element_type=F32)


def _dot_nt(a, b):
    return lax.dot_general(a, b, (((1,), (1,)), ((), ())), preferred_element_type=F32)


def _dot_tn(a, b):
    return lax.dot_general(a, b, (((0,), (0,)), ((), ())), preferred_element_type=F32)


def _layer_norm(r, g, b):
    mu = jnp.mean(r, -1, keepdims=True)
    d = r - mu
    var = jnp.mean(d * d, -1, keepdims=True)
    return d * lax.rsqrt(var + LN_EPS) * g + b


def _rms_norm(x, g):
    return x * lax.rsqrt(jnp.mean(x * x, -1, keepdims=True) + RMS_EPS) * g


def _silu(x):
    return x * jax.nn.sigmoid(x)


def _mm_body(x_ref, w_ref, o_ref):
    o_ref[...] = _dot(x_ref[...].astype(BF16), w_ref[...]).astype(o_ref.dtype)


def _matmul(x, w, out_dtype, tm, tn):
    M, K = x.shape
    N = w.shape[1]
    return pl.pallas_call(
        _mm_body,
        name="matmul",
        grid=(M // tm, N // tn),
        in_specs=[pl.BlockSpec((tm, K), lambda i, j: (i, 0)),
                  pl.BlockSpec((K, tn), lambda i, j: (0, j))],
        out_specs=pl.BlockSpec((tm, tn), lambda i, j: (i, j)),
        out_shape=jax.ShapeDtypeStruct((M, N), out_dtype),
        compiler_params=_params("parallel", "parallel"),
    )(x, w)


def _outproj_body(mix_ref, mem_ref, wa_ref, wb_ref, x_ref, g_ref, b_ref, o_ref, ob_ref, *, alpha):
    y = _dot(mix_ref[...], wa_ref[...]) + _dot(mem_ref[...], wb_ref[...])
    o = _layer_norm(alpha * x_ref[...] + y, g_ref[...], b_ref[...])
    o_ref[...] = o
    ob_ref[...] = o.astype(BF16)


def _outproj_ln(mix, memo, w_mix, w_mem, x, g, b, alpha, tm=512):
    M, D = x.shape
    return pl.pallas_call(
        functools.partial(_outproj_body, alpha=alpha),
        name="outproj_ln",
        grid=(M // tm,),
        in_specs=[pl.BlockSpec((tm, mix.shape[1]), lambda i: (i, 0)),
                  pl.BlockSpec((tm, memo.shape[1]), lambda i: (i, 0)),
                  pl.BlockSpec(w_mix.shape, lambda i: (0, 0)),
                  pl.BlockSpec(w_mem.shape, lambda i: (0, 0)),
                  pl.BlockSpec((tm, D), lambda i: (i, 0)),
                  pl.BlockSpec((1, D), lambda i: (0, 0)),
                  pl.BlockSpec((1, D), lambda i: (0, 0))],
        out_specs=[pl.BlockSpec((tm, D), lambda i: (i, 0))] * 2,
        out_shape=[jax.ShapeDtypeStruct((M, D), F32), jax.ShapeDtypeStruct((M, D), BF16)],
        compiler_params=_params("parallel"),
    )(mix, memo, w_mix, w_mem, x, g, b)


def _mlp_body(x_ref, xb_ref, w1_ref, w2_ref, g_ref, b_ref, o_ref, ob_ref, acc_ref, *, alpha):
    f = pl.program_id(1)

    @pl.when(f == 0)
    def _():
        acc_ref[...] = jnp.zeros_like(acc_ref)

    h = jnp.maximum(_dot(xb_ref[...], w1_ref[...]), 0.0)
    acc_ref[...] += _dot((h * h).astype(BF16), w2_ref[...])

    @pl.when(f == pl.num_programs(1) - 1)
    def _():
        o = _layer_norm(alpha * x_ref[...] + acc_ref[...], g_ref[...], b_ref[...])
        o_ref[...] = o
        ob_ref[...] = o.astype(BF16)


def _mlp_ln(x, xb, w1, w2, g, b, alpha, tm=512, tf=1024):
    M, D = x.shape
    FF = w1.shape[1]
    return pl.pallas_call(
        functools.partial(_mlp_body, alpha=alpha),
        name="mlp_ln",
        grid=(M // tm, FF // tf),
        in_specs=[pl.BlockSpec((tm, D), lambda i, f: (i, 0)),
                  pl.BlockSpec((tm, D), lambda i, f: (i, 0)),
                  pl.BlockSpec((D, tf), lambda i, f: (0, f)),
                  pl.BlockSpec((tf, D), lambda i, f: (f, 0)),
                  pl.BlockSpec((1, D), lambda i, f: (0, 0)),
                  pl.BlockSpec((1, D), lambda i, f: (0, 0))],
        out_specs=[pl.BlockSpec((tm, D), lambda i, f: (i, 0))] * 2,
        out_shape=[jax.ShapeDtypeStruct((M, D), F32), jax.ShapeDtypeStruct((M, D), BF16)],
        scratch_shapes=[pltpu.VMEM((tm, D), F32)],
        compiler_params=_params("parallel", "arbitrary"),
    )(x, xb, w1, w2, g, b)


def _rope_body(pos_ref, freq_ref, cos_ref, sin_ref):
    ang = pos_ref[...].astype(F32) * freq_ref[...]
    lane = lax.broadcasted_iota(jnp.int32, ang.shape, 1)
    keep = lane < QK_ROPE
    cos_ref[...] = jnp.where(keep, jnp.cos(ang), 0.0)
    sin_ref[...] = jnp.where(keep, jnp.sin(ang), 0.0)


def _rope_tables(pos, freq, tm=1024):
    T = pos.shape[0]
    shp = jax.ShapeDtypeStruct((T, LANE), F32)
    return pl.pallas_call(
        _rope_body,
        name="rope_tables",
        grid=(T // tm,),
        in_specs=[pl.BlockSpec((tm, 1), lambda i: (i, 0)),
                  pl.BlockSpec((1, LANE), lambda i: (0, 0))],
        out_specs=[pl.BlockSpec((tm, LANE), lambda i: (i, 0))] * 2,
        out_shape=[shp, shp],
        compiler_params=_params("parallel"),
    )(pos, freq)


def _qup_body(c_ref, g_ref, w_ref, cos_ref, sin_ref, o_ref, *, scale):
    cn = _rms_norm(c_ref[...], g_ref[...]).astype(BF16)
    r = _dot(cn, w_ref[...])
    cos = cos_ref[...] * scale
    sin = sin_ref[...] * scale
    hw = 2 * LANE
    rot0 = N_MIX_HEADS * hw
    for h in range(N_MIX_HEADS):
        o_ref[:, h * hw:h * hw + LANE] = (r[:, h * hw:h * hw + LANE] * scale).astype(o_ref.dtype)
        rope = r[:, h * hw + LANE:(h + 1) * hw] * cos + r[:, rot0 + h * LANE:rot0 + (h + 1) * LANE] * sin
        o_ref[:, h * hw + LANE:(h + 1) * hw] = rope.astype(o_ref.dtype)


def _q_up(h, g, w, cosp, sinp, scale, tm=256):
    T = h.shape[0]
    N = N_MIX_HEADS * 2 * LANE
    return pl.pallas_call(
        functools.partial(_qup_body, scale=scale),
        name="mla_q_up",
        grid=(T // tm,),
        in_specs=[pl.BlockSpec((tm, Q_LORA), lambda i: (i, 0)),
                  pl.BlockSpec((1, Q_LORA), lambda i: (0, 0)),
                  pl.BlockSpec(w.shape, lambda i: (0, 0)),
                  pl.BlockSpec((tm, LANE), lambda i: (i, 0)),
                  pl.BlockSpec((tm, LANE), lambda i: (i, 0))],
        out_specs=pl.BlockSpec((tm, N), lambda i: (i, 0)),
        out_shape=jax.ShapeDtypeStruct((T, N), BF16),
        compiler_params=_params("parallel"),
    )(h, g, w, cosp, sinp)


def _kvup_body(c_ref, kr_ref, g_ref, w_ref, cos_ref, sin_ref, kn_ref, v_ref, krp_ref):
    cn = _rms_norm(c_ref[...], g_ref[...]).astype(BF16)
    r = _dot(cn, w_ref[...])
    kn_ref[...] = r[:, :MIX_WIDTH].astype(kn_ref.dtype)
    v_ref[...] = r[:, MIX_WIDTH:].astype(v_ref.dtype)
    kr = kr_ref[...]
    swapped = pltpu.roll(kr, QK_ROPE, 1)
    krp_ref[...] = (kr * cos_ref[...] + swapped * sin_ref[...]).astype(krp_ref.dtype)


def _kv_up(h, h_tail, g, w, cosp, sinp, tm=256):
    T = h.shape[0]
    kr_blk = MEM_WIDTH // LANE
    return pl.pallas_call(
        _kvup_body,
        name="mla_kv_up",
        grid=(T // tm,),
        in_specs=[pl.BlockSpec((tm, KV_LORA), lambda i: (i, 1)),
                  pl.BlockSpec((tm, LANE), lambda i: (i, kr_blk)),
                  pl.BlockSpec((1, KV_LORA), lambda i: (0, 0)),
                  pl.BlockSpec(w.shape, lambda i: (0, 0)),
                  pl.BlockSpec((tm, LANE), lambda i: (i, 0)),
                  pl.BlockSpec((tm, LANE), lambda i: (i, 0))],
        out_specs=[pl.BlockSpec((tm, MIX_WIDTH), lambda i: (i, 0)),
                   pl.BlockSpec((tm, MIX_WIDTH), lambda i: (i, 0)),
                   pl.BlockSpec((tm, LANE), lambda i: (i, 0))],
        out_shape=[jax.ShapeDtypeStruct((T, MIX_WIDTH), BF16),
                   jax.ShapeDtypeStruct((T, MIX_WIDTH), BF16),
                   jax.ShapeDtypeStruct((T, LANE), BF16)],
        compiler_params=_params("parallel"),
    )(h, h_tail, g, w, cosp, sinp)


def _flash_body(q_ref, kn_ref, kr_ref, v_ref, o_ref, *, tq, nh):
    qi = pl.program_id(2)
    qw = 2 * LANE

    hs = range(nh)
    hd = [slice(h * HEAD_DIM, (h + 1) * HEAD_DIM) for h in hs]

    def scores(j):
        start = pl.multiple_of(j * tq, tq)
        kr = kr_ref[0, pl.ds(start, tq), :]
        ks = [jnp.concatenate([kn_ref[0, pl.ds(start, tq), hd[h]], kr], axis=1) for h in hs]
        return [_dot_nt(q_ref[0, :, h * qw:(h + 1) * qw], ks[h]) for h in hs]

    def update(j, s, m, l, acc):
        start = pl.multiple_of(j * tq, tq)
        m_new = [jnp.maximum(m[h], jnp.max(s[h], -1, keepdims=True)) for h in hs]
        a = [jnp.exp2(m[h] - m_new[h]) for h in hs]
        p = [jnp.exp2(s[h] - m_new[h]) for h in hs]
        l = [a[h] * l[h] + jnp.sum(p[h], -1, keepdims=True) for h in hs]
        pv = [_dot(p[h].astype(BF16), v_ref[0, pl.ds(start, tq), hd[h]]) for h in hs]
        acc = [a[h] * acc[h] + pv[h] for h in hs]
        return m_new, l, acc

    def block(j, carry):
        return update(j, scores(j), *carry)

    def pair(j2, carry):
        return block(2 * j2 + 1, block(2 * j2, carry))

    init = ([jnp.full((tq, 1), -jnp.inf, F32)] * nh,
            [jnp.zeros((tq, 1), F32)] * nh,
            [jnp.zeros((tq, HEAD_DIM), F32)] * nh)
    carry = lax.fori_loop(0, qi // 2, pair, init)
    carry = lax.cond(qi % 2 == 1, lambda c: block(qi - 1, c), lambda c: c, carry)
    row = lax.broadcasted_iota(jnp.int32, (tq, tq), 0)
    col = lax.broadcasted_iota(jnp.int32, (tq, tq), 1)
    s = [jnp.where(col <= row, sh, -jnp.inf) for sh in scores(qi)]
    _, l, acc = update(qi, s, *carry)
    o_ref[0] = jnp.concatenate([acc[h] / l[h] for h in hs], axis=1).astype(o_ref.dtype)


def _flash(q, kn, kr, v, tq=512, nh=2):
    B, S, _ = q.shape
    return pl.pallas_call(
        functools.partial(_flash_body, tq=tq, nh=nh),
        name="mla_flash",
        grid=(B, N_MIX_HEADS // nh, S // tq),
        in_specs=[pl.BlockSpec((1, tq, nh * 2 * LANE), lambda b, h, i: (b, i, h)),
                  pl.BlockSpec((1, S, nh * HEAD_DIM), lambda b, h, i: (b, 0, h)),
                  pl.BlockSpec((1, S, LANE), lambda b, h, i: (b, 0, 0)),
                  pl.BlockSpec((1, S, nh * HEAD_DIM), lambda b, h, i: (b, 0, h))],
        out_specs=pl.BlockSpec((1, tq, nh * HEAD_DIM), lambda b, h, i: (b, i, h)),
        out_shape=jax.ShapeDtypeStruct((B, S, MIX_WIDTH), BF16),
        compiler_params=_params("parallel", "parallel", "arbitrary"),
    )(q, kn, kr, v)


def _memattn_body(q_ref, kv_ref, o_ref, *, scale):
    for h in range(N_MEM_HEADS):
        q = (q_ref[0, :, h * HEAD_DIM:(h + 1) * HEAD_DIM] * scale).astype(BF16)
        k = kv_ref[0, :, h * HEAD_DIM:(h + 1) * HEAD_DIM].astype(BF16)
        v = kv_ref[0, :, MEM_WIDTH + h * HEAD_DIM:MEM_WIDTH + (h + 1) * HEAD_DIM].astype(BF16)
        s = _dot_nt(q, k)
        p = jnp.exp(s - jnp.max(s, -1, keepdims=True))
        o = _dot(p.astype(BF16), v) / jnp.sum(p, -1, keepdims=True)
        o_ref[0, :, h * HEAD_DIM:(h + 1) * HEAD_DIM] = o.astype(o_ref.dtype)


def _mem_attention(h3, mem_kv, tm=512):
    B, S, _ = h3.shape
    M = mem_kv.shape[1]
    return pl.pallas_call(
        functools.partial(_memattn_body, scale=HEAD_DIM ** -0.5),
        name="mem_attention",
        grid=(B, S // tm),
        in_specs=[pl.BlockSpec((1, tm, MEM_WIDTH), lambda b, i: (b, i, 0)),
                  pl.BlockSpec((1, M, 2 * MEM_WIDTH), lambda b, i: (b, 0, 0))],
        out_specs=pl.BlockSpec((1, tm, MEM_WIDTH), lambda b, i: (b, i, 0)),
        out_shape=jax.ShapeDtypeStruct((B, S, MEM_WIDTH), BF16),
        compiler_params=_params("parallel", "parallel"),
    )(h3, mem_kv)


def _gate_body(ab_ref, alog_ref, dt_ref, g_ref, beta_ref):
    ab = ab_ref[...]
    g_ref[...] = -jnp.exp(alog_ref[...]) * jax.nn.softplus(ab + dt_ref[...])
    beta_ref[...] = jax.nn.sigmoid(ab)


def _gdn_gates(h, col_blk, alog, dt, tm=1024):
    T = h.shape[0]
    shp = jax.ShapeDtypeStruct((T, LANE), F32)
    return pl.pallas_call(
        _gate_body,
        name="gdn_gates",
        grid=(T // tm,),
        in_specs=[pl.BlockSpec((tm, LANE), lambda i: (i, col_blk)),
                  pl.BlockSpec((1, LANE), lambda i: (0, 0)),
                  pl.BlockSpec((1, LANE), lambda i: (0, 0))],
        out_specs=[pl.BlockSpec((tm, LANE), lambda i: (i, 0))] * 2,
        out_shape=[shp, shp],
        compiler_params=_params("parallel"),
    )(h, alog, dt)


def _unit_lower_inverses(Ls, masks):
    eye, neg_strict16, blk32_only, blk64_only = masks
    idx = range(len(Ls))

    def bf(xs):
        return [x.astype(BF16) for x in xs]

    n1 = [L * neg_strict16 for L in Ls]
    nb = bf(n1)
    p = [eye + x for x in n1]
    for _ in range(3):
        nb = bf([_dot(nb[i], nb[i]) for i in idx])
        pb = bf(p)
        p = [p[i] + _dot(pb[i], nb[i]) for i in idx]
    for level in (blk32_only, blk64_only):
        pb = bf(p)
        cb = bf([L * level for L in Ls])
        t = bf([_dot(pb[i], cb[i]) for i in idx])
        p = [p[i] - _dot(t[i], pb[i]) for i in idx]
    return p


def _causal_conv_silu(x_ref, ln, prev, w):
    R = x_ref.shape[1]
    taps = [w[CONV_WIDTH - 1 - d:CONV_WIDTH - d, :] for d in range(CONV_WIDTH)]
    body = x_ref[0, SUBLANE:, ln] * taps[0]
    for d in range(1, CONV_WIDTH):
        body = body + x_ref[0, pl.ds(SUBLANE - d, R - SUBLANE), ln] * taps[d]
    xe = jnp.concatenate([prev, x_ref[0, :SUBLANE, ln]], axis=0)
    head = xe[SUBLANE:] * taps[0]
    for d in range(1, CONV_WIDTH):
        head = head + pltpu.roll(xe, d, 0)[SUBLANE:] * taps[d]
    return _silu(jnp.concatenate([head, body], axis=0))


def _l2_normalize(x):
    return x * lax.rsqrt(jnp.sum(x * x, -1, keepdims=True) + L2_EPS)


def _gdn_body(q_ref, k_ref, v_ref, z_ref, qp_ref, kp_ref, vp_ref, wq_ref, wk_ref, wv_ref,
              gate_ref, onorm_ref, mtril_ref, masks_ref, o_ref, state_ref):
    R = q_ref.shape[1]
    nc = R // CHUNK
    step = pl.program_id(2)

    @pl.when(step == 0)
    def _():
        state_ref[...] = jnp.zeros_like(state_ref)

    not_first = (step > 0).astype(F32)
    row = lax.broadcasted_iota(jnp.int32, (R, R), 0)
    col = lax.broadcasted_iota(jnp.int32, (R, R), 1)

    tril = jnp.logical_and((row >> 6) == (col >> 6), col <= row)
    m_tril = mtril_ref[...]
    masks = tuple(masks_ref[i] for i in range(4))

    def split3(a):
        a1 = a.astype(BF16)
        r1 = a - a1.astype(F32)
        a2 = r1.astype(BF16)
        a3 = (r1 - a2.astype(F32)).astype(BF16)
        return a1, a2, a3

    def lane_bcast(a, lane):
        return jnp.broadcast_to(a[:, lane:lane + 1], (R, HEAD_DIM))

    gates = gate_ref[0]
    gparts = split3(gates)
    gc_all = _dot(m_tril, gparts[0]) + _dot(m_tril, gparts[1]) + _dot(m_tril, gparts[2])
    gc_rows = gc_all.T
    g_end_all = jnp.concatenate(
        [jnp.broadcast_to(gc_all[(c + 1) * CHUNK - 1:(c + 1) * CHUNK, :], (CHUNK, LANE)) for c in range(nc)], axis=0)
    egc_all = jnp.exp(gc_all)
    kdec_all = jnp.exp(g_end_all - gc_all)
    glast_all = jnp.exp(g_end_all)

    heads = range(GDN_HEADS)
    lns = [slice(hh * HEAD_DIM, (hh + 1) * HEAD_DIM) for hh in heads]
    q, k, v, beta, egc, decay, kb, kbf, lmat = [], [], [], [], [], [], [], [], []
    for hh, ln in zip(heads, lns):
        xq = _causal_conv_silu(q_ref, ln, qp_ref[0, :, ln] * not_first, wq_ref[:, ln])
        xk = _causal_conv_silu(k_ref, ln, kp_ref[0, :, ln] * not_first, wk_ref[:, ln])
        v.append(_causal_conv_silu(v_ref, ln, vp_ref[0, :, ln] * not_first, wv_ref[:, ln]))
        q.append(_l2_normalize(xq) * (HEAD_DIM ** -0.5))
        k.append(_l2_normalize(xk))
        gc = lane_bcast(gc_all, hh)
        beta.append(lane_bcast(gates, GDN_HEADS + hh))
        egc.append(lane_bcast(egc_all, hh))
        diff = jnp.concatenate([gc] * (R // HEAD_DIM), axis=1) - gc_rows[hh:hh + 1, :]
        decay.append(jnp.exp(jnp.where(tril, diff, MASKED_LOG)))
        kb.append(k[hh] * beta[hh])
        kbf.append(k[hh].astype(BF16))
        lmat.append(_dot_nt(kb[hh].astype(BF16), kbf[hh]) * decay[hh])

    tinv = _unit_lower_inverses(lmat, masks)

    u, w, a_qk, q_dec, k_dec = [], [], [], [], []
    for hh in heads:
        rhs = jnp.concatenate([v[hh] * beta[hh], kb[hh] * egc[hh]], axis=1).astype(BF16)
        sol = _dot(tinv[hh].astype(BF16), rhs)
        u.append(sol[:, :HEAD_DIM])
        w.append(sol[:, HEAD_DIM:].astype(BF16))
        a_qk.append((_dot_nt(q[hh].astype(BF16), kbf[hh]) * decay[hh]).astype(BF16))
        q_dec.append((q[hh] * egc[hh]).astype(BF16))
        k_dec.append((k[hh] * lane_bcast(kdec_all, hh)).astype(BF16))

    state = [state_ref[hh] for hh in heads]
    v_new = [[] for _ in heads]
    o_inter = [[] for _ in heads]
    for c in range(nc):
        rows = slice(c * CHUNK, (c + 1) * CHUNK)
        for hh in heads:
            both = _dot(jnp.concatenate([w[hh][rows], q_dec[hh][rows]], axis=0), state[hh].astype(BF16))
            vn = u[hh][rows] - both[:CHUNK]
            o_inter[hh].append(both[CHUNK:])
            g_last = glast_all[c * CHUNK:c * CHUNK + 1, hh:hh + 1]
            state[hh] = state[hh] * g_last + _dot_tn(k_dec[hh][rows], vn.astype(BF16))
            v_new[hh].append(vn)
    state_ref[...] = jnp.stack(state)

    outs = []
    for hh, ln in zip(heads, lns):
        vn_all = jnp.concatenate(v_new[hh], axis=0).astype(BF16)
        o = jnp.concatenate(o_inter[hh], axis=0) + _dot(a_qk[hh], vn_all)
        outs.append(_rms_norm(o, onorm_ref[...]) * _silu(z_ref[0, :, ln]))
    o_ref[0] = jnp.concatenate(outs, axis=1).astype(o_ref.dtype)


def _gdn_block_masks(R):
    row = lax.broadcasted_iota(jnp.int32, (R, R), 0)
    col = lax.broadcasted_iota(jnp.int32, (R, R), 1)

    def same(n):
        return (row // n) == (col // n)

    one = lambda c: jnp.where(c, 1.0, 0.0).astype(F32)
    tril = one(jnp.logical_and(same(CHUNK), col <= row)).astype(BF16)
    eye = one(row == col)
    neg_strict16 = -one(jnp.logical_and(same(16), col < row))
    blk32_only = one(jnp.logical_and(same(32), jnp.logical_not(same(16))))
    blk64_only = one(jnp.logical_and(same(CHUNK), jnp.logical_not(same(32))))
    return tril, jnp.stack([eye, neg_strict16, blk32_only, blk64_only])


def _gdn_mix(h3, conv_w, gates, o_norm):
    B, S, _ = h3.shape
    R, G = GDN_ROWS, GDN_HEADS
    W = G * HEAD_DIM
    ng = N_MIX_HEADS // G
    rb = R // SUBLANE

    def cur(off):
        return pl.BlockSpec((1, R, W), lambda b, g, i: (b, i, off * ng + g))

    def prev(off):
        return pl.BlockSpec((1, SUBLANE, W), lambda b, g, i: (b, jnp.maximum(i * rb - 1, 0), off * ng + g))

    def convw(off):
        return pl.BlockSpec((CONV_WIDTH, W), lambda b, g, i: (0, off * ng + g))

    return pl.pallas_call(
        _gdn_body,
        name="gdn_delta_rule",
        grid=(B, ng, S // R),
        in_specs=[cur(0), cur(1), cur(2), cur(3), prev(0), prev(1), prev(2),
                  convw(0), convw(1), convw(2),
                  pl.BlockSpec((1, R, LANE), lambda b, g, i: (b, i, g)),
                  pl.BlockSpec((1, HEAD_DIM), lambda b, g, i: (0, 0)),
                  pl.BlockSpec((R, R), lambda b, g, i: (0, 0)),
                  pl.BlockSpec((4, R, R), lambda b, g, i: (0, 0, 0))],
        out_specs=pl.BlockSpec((1, R, W), lambda b, g, i: (b, i, g)),
        out_shape=jax.ShapeDtypeStruct((B, S, MIX_WIDTH), BF16),
        scratch_shapes=[pltpu.VMEM((G, HEAD_DIM, HEAD_DIM), F32)],
        compiler_params=_params("parallel", "parallel", "arbitrary"),
    )(h3, h3, h3, h3, h3, h3, h3, conv_w, conv_w, conv_w, gates, o_norm, *_gdn_block_masks(R))


def _rotate_half_cols(w):
    half = w.shape[-1] // 2
    return jnp.concatenate([-w[..., half:], w[..., :half]], -1)


def _mla_weights(w_in, w_uq, w_ukv):
    w_main = w_in[:, :Q_LORA + KV_LORA].astype(BF16)
    k_r = w_in[:, Q_LORA + KV_LORA:Q_LORA + KV_LORA + QK_ROPE]
    q_mem = w_in[:, Q_LORA + KV_LORA + QK_ROPE:]
    w_tail = jnp.concatenate([q_mem, k_r, _rotate_half_cols(k_r)], 1).astype(BF16)

    uq = w_uq.reshape(Q_LORA, N_MIX_HEADS, QK_NOPE + QK_ROPE)
    nope, rope = uq[..., :QK_NOPE], uq[..., QK_NOPE:]
    zeros = jnp.zeros_like(rope)
    main = jnp.concatenate([nope, rope, zeros], -1).reshape(Q_LORA, -1)
    rot = jnp.concatenate([_rotate_half_cols(rope), zeros], -1).reshape(Q_LORA, -1)
    w_uq_p = jnp.concatenate([main, rot], 1).astype(BF16)

    ukv = w_ukv.reshape(KV_LORA, N_MIX_HEADS, QK_NOPE + HEAD_DIM)
    w_ukv_p = jnp.concatenate([ukv[..., :QK_NOPE].reshape(KV_LORA, -1),
                               ukv[..., QK_NOPE:].reshape(KV_LORA, -1)], 1).astype(BF16)
    return w_main, w_tail, w_uq_p, w_ukv_p


def _gdn_in_weights(w_in):
    D = w_in.shape[0]
    w_main = w_in[:, :4 * MIX_WIDTH].astype(BF16)
    ab = w_in[:, 4 * MIX_WIDTH:4 * MIX_WIDTH + 2 * N_MIX_HEADS]
    q_mem = w_in[:, 4 * MIX_WIDTH + 2 * N_MIX_HEADS:]
    pad = jnp.zeros((D, LANE - 2 * N_MIX_HEADS), w_in.dtype)
    return w_main, jnp.concatenate([q_mem, ab, pad], 1).astype(BF16)


def _pad_lanes(v):
    return jnp.pad(v, (0, LANE - v.shape[0]))[None, :]


def kernel(x, mem, positions, mla_w_in, mla_q_norm, mla_w_uq, mla_kv_norm, mla_w_ukv, gdn_w_in, gdn_conv, gdn_a_log, gdn_dt_bias, gdn_o_norm, mem_w_kv, w_out, ln1_g, ln1_b, mlp_w1, mlp_w2, ln2_g, ln2_b):
    B, S, D = x.shape
    T = B * S
    depth = w_out.shape[0]
    alpha = (2 * depth) ** 0.25
    G = GDN_HEADS
    ng = N_MIX_HEADS // G

    inv_freq = 1.0 / (ROPE_THETA ** (jnp.arange(0, QK_ROPE, 2, dtype=F32) / QK_ROPE))
    freq = _pad_lanes(jnp.concatenate([inv_freq, inv_freq]))
    cosp, sinp = _rope_tables(positions.reshape(T, 1), freq)

    xf = x.reshape(T, D)
    xb = xf
    mem2 = mem.reshape(B * mem.shape[1], D)
    for i in range(depth):
        j = i // 2
        mem_kv = _matmul(mem2, mem_w_kv[i].astype(BF16), F32, tm=256, tn=512)
        mem_kv = mem_kv.reshape(B, mem.shape[1], 2 * MEM_WIDTH)
        if i % 2 == 0:
            w_main, w_tail, w_uq_p, w_ukv_p = _mla_weights(mla_w_in[j], mla_w_uq[j], mla_w_ukv[j])
            h = _matmul(xb, w_main, F32, tm=1024, tn=w_main.shape[1])
            h_tail = _matmul(xb, w_tail, F32, tm=1024, tn=w_tail.shape[1])
            q = _q_up(h, mla_q_norm[j][None, :], w_uq_p, cosp, sinp, (QK_NOPE + QK_ROPE) ** -0.5 * LOG2_E)
            kn, v, krp = _kv_up(h, h_tail, mla_kv_norm[j][None, :], w_ukv_p, cosp, sinp)
            mix = _flash(q.reshape(B, S, -1), kn.reshape(B, S, -1), krp.reshape(B, S, -1),
                         v.reshape(B, S, -1)).reshape(T, MIX_WIDTH)
        else:
            w_main, w_tail = _gdn_in_weights(gdn_w_in[j])
            h = _matmul(xb, w_main, F32, tm=1024, tn=MIX_WIDTH)
            h_tail = _matmul(xb, w_tail, F32, tm=1024, tn=w_tail.shape[1])
            alog = _pad_lanes(gdn_a_log[j])
            dt = _pad_lanes(gdn_dt_bias[j])
            g_all, beta_all = _gdn_gates(h_tail, MEM_WIDTH // LANE, alog, dt)
            g = g_all[:, :N_MIX_HEADS].reshape(B, S, ng, G)
            beta = beta_all[:, N_MIX_HEADS:2 * N_MIX_HEADS].reshape(B, S, ng, G)
            gb = jnp.concatenate([g, beta], -1)
            gates = jnp.pad(gb, ((0, 0), (0, 0), (0, 0), (0, LANE - 2 * G))).reshape(B, S, ng * LANE)
            h3 = h.reshape(B, S, -1)
            mix = _gdn_mix(h3, gdn_conv[j], gates, gdn_o_norm[j][None, :]).reshape(T, MIX_WIDTH)
        mem_o = _mem_attention(h_tail.reshape(B, S, -1), mem_kv)
        wo = w_out[i].astype(BF16)
        xf, xb = _outproj_ln(mix, mem_o.reshape(T, MEM_WIDTH), wo[:MIX_WIDTH], wo[MIX_WIDTH:], xf,
                             ln1_g[i][None, :], ln1_b[i][None, :], alpha)
        xf, xb = _mlp_ln(xf, xb, mlp_w1[i].astype(BF16), mlp_w2[i].astype(BF16),
                         ln2_g[i][None, :], ln2_b[i][None, :], alpha)
    return xf.reshape(B, S, D)
```

```python
import functools

import jax
import jax.numpy as jnp
from jax import lax
from jax.experimental import pallas as pl
from jax.experimental.pallas import tpu as pltpu

F32 = jnp.float32
BF16 = jnp.bfloat16

HEAD_DIM = 128
N_MIX_HEADS = 12
N_MEM_HEADS = 4
MIX_WIDTH = N_MIX_HEADS * HEAD_DIM
MEM_WIDTH = N_MEM_HEADS * HEAD_DIM
Q_LORA = 512
KV_LORA = 512
QK_NOPE = 128
QK_ROPE = 64
ROPE_THETA = 10000.0
CONV_WIDTH = 4
CHUNK = 64
LN_EPS = 1e-5
RMS_EPS = 1e-6
L2_EPS = 1e-6

LANE = 128
SUBLANE = 8
VMEM_LIMIT = 56 * 1024 * 1024

GDN_ROWS = 256
GDN_HEADS = 4
LOG2_E = 1.4426950408889634
MASKED_LOG = -1e30


def _params(*sem):
    return pltpu.CompilerParams(dimension_semantics=sem, vmem_limit_bytes=VMEM_LIMIT)


def _dot(a, b):
    return jnp.dot(a, b, preferred_element_type=F32)


def _dot_nt(a, b):
    return lax.dot_general(a, b, (((1,), (1,)), ((), ())), preferred_element_type=F32)


def _dot_tn(a, b):
    return lax.dot_general(a, b, (((0,), (0,)), ((), ())), preferred_element_type=F32)


def _layer_norm(r, g, b):
    mu = jnp.mean(r, -1, keepdims=True)
    d = r - mu
    var = jnp.mean(d * d, -1, keepdims=True)
    return d * lax.rsqrt(var + LN_EPS) * g + b


def _rms_norm(x, g):
    return x * lax.rsqrt(jnp.mean(x * x, -1, keepdims=True) + RMS_EPS) * g


def _silu(x):
    return x * jax.nn.sigmoid(x)


def _mm_body(x_ref, w_ref, o_ref):
    o_ref[...] = _dot(x_ref[...].astype(BF16), w_ref[...]).astype(o_ref.dtype)


def _matmul(x, w, out_dtype, tm, tn, layer=None, n_cols=None):
    M, K = x.shape
    if layer is None:
        N = w.shape[1]
        w_spec = pl.BlockSpec((K, tn), lambda i, j: (0, j))
    else:
        N = n_cols
        w_spec = pl.BlockSpec((None, K, tn), lambda i, j: (layer, 0, j))
    return pl.pallas_call(
        _mm_body,
        name="matmul",
        grid=(M // tm, N // tn),
        in_specs=[pl.BlockSpec((tm, K), lambda i, j: (i, 0)), w_spec],
        out_specs=pl.BlockSpec((tm, tn), lambda i, j: (i, j)),
        out_shape=jax.ShapeDtypeStruct((M, N), out_dtype),
        compiler_params=_params("parallel", "parallel"),
    )(x, w)


def _outproj_body(mix_ref, mem_ref, wa_ref, wb_ref, x_ref, g_ref, b_ref, o_ref, ob_ref, *, alpha):
    y = _dot(mix_ref[...], wa_ref[...]) + _dot(mem_ref[...], wb_ref[...])
    o = _layer_norm(alpha * x_ref[...] + y, g_ref[...], b_ref[...])
    o_ref[...] = o
    ob_ref[...] = o.astype(BF16)


def _outproj_ln(mix, memo, w_all, layer, x, g, b, alpha, tm=512):
    M, D = x.shape
    mem_row_blk = mix.shape[1] // memo.shape[1]
    return pl.pallas_call(
        functools.partial(_outproj_body, alpha=alpha),
        name="outproj_ln",
        grid=(M // tm,),
        in_specs=[pl.BlockSpec((tm, mix.shape[1]), lambda i: (i, 0)),
                  pl.BlockSpec((tm, memo.shape[1]), lambda i: (i, 0)),
                  pl.BlockSpec((None, mix.shape[1], D), lambda i: (layer, 0, 0)),
                  pl.BlockSpec((None, memo.shape[1], D), lambda i: (layer, mem_row_blk, 0)),
                  pl.BlockSpec((tm, D), lambda i: (i, 0)),
                  pl.BlockSpec((1, D), lambda i: (0, 0)),
                  pl.BlockSpec((1, D), lambda i: (0, 0))],
        out_specs=[pl.BlockSpec((tm, D), lambda i: (i, 0))] * 2,
        out_shape=[jax.ShapeDtypeStruct((M, D), F32), jax.ShapeDtypeStruct((M, D), BF16)],
        compiler_params=_params("parallel"),
    )(mix, memo, w_all, w_all, x, g, b)


def _mlp_body(x_ref, xb_ref, w1_ref, w2_ref, g_ref, b_ref, o_ref, ob_ref, acc_ref, *, alpha):
    f = pl.program_id(1)

    @pl.when(f == 0)
    def _():
        acc_ref[...] = jnp.zeros_like(acc_ref)

    h = jnp.maximum(_dot(xb_ref[...], w1_ref[...]), 0.0)
    acc_ref[...] += _dot((h * h).astype(BF16), w2_ref[...])

    @pl.when(f == pl.num_programs(1) - 1)
    def _():
        o = _layer_norm(alpha * x_ref[...] + acc_ref[...], g_ref[...], b_ref[...])
        o_ref[...] = o
        ob_ref[...] = o.astype(BF16)


def _mlp_ln(x, xb, w1, w2, layer, g, b, alpha, tm=512, tf=1024):
    M, D = x.shape
    FF = w1.shape[2]
    return pl.pallas_call(
        functools.partial(_mlp_body, alpha=alpha),
        name="mlp_ln",
        grid=(M // tm, FF // tf),
        in_specs=[pl.BlockSpec((tm, D), lambda i, f: (i, 0)),
                  pl.BlockSpec((tm, D), lambda i, f: (i, 0)),
                  pl.BlockSpec((None, D, tf), lambda i, f: (layer, 0, f)),
                  pl.BlockSpec((None, tf, D), lambda i, f: (layer, f, 0)),
                  pl.BlockSpec((1, D), lambda i, f: (0, 0)),
                  pl.BlockSpec((1, D), lambda i, f: (0, 0))],
        out_specs=[pl.BlockSpec((tm, D), lambda i, f: (i, 0))] * 2,
        out_shape=[jax.ShapeDtypeStruct((M, D), F32), jax.ShapeDtypeStruct((M, D), BF16)],
        scratch_shapes=[pltpu.VMEM((tm, D), F32)],
        compiler_params=_params("parallel", "arbitrary"),
    )(x, xb, w1, w2, g, b)


def _rope_body(pos_ref, freq_ref, cos_ref, sin_ref):
    ang = pos_ref[...].astype(F32) * freq_ref[...]
    lane = lax.broadcasted_iota(jnp.int32, ang.shape, 1)
    keep = lane < QK_ROPE
    cos_ref[...] = jnp.where(keep, jnp.cos(ang), 0.0)
    sin_ref[...] = jnp.where(keep, jnp.sin(ang), 0.0)


def _rope_tables(pos, freq, tm=1024):
    T = pos.shape[0]
    shp = jax.ShapeDtypeStruct((T, LANE), F32)
    return pl.pallas_call(
        _rope_body,
        name="rope_tables",
        grid=(T // tm,),
        in_specs=[pl.BlockSpec((tm, 1), lambda i: (i, 0)),
                  pl.BlockSpec((1, LANE), lambda i: (0, 0))],
        out_specs=[pl.BlockSpec((tm, LANE), lambda i: (i, 0))] * 2,
        out_shape=[shp, shp],
        compiler_params=_params("parallel"),
    )(pos, freq)


def _qup_body(c_ref, g_ref, w_ref, cos_ref, sin_ref, o_ref, *, scale):
    cn = _rms_norm(c_ref[...], g_ref[...]).astype(BF16)
    r = _dot(cn, w_ref[...])
    cos = cos_ref[...] * scale
    sin = sin_ref[...] * scale
    hw = 2 * LANE
    rot0 = N_MIX_HEADS * hw
    for h in range(N_MIX_HEADS):
        o_ref[:, h * hw:h * hw + LANE] = (r[:, h * hw:h * hw + LANE] * scale).astype(o_ref.dtype)
        rope = r[:, h * hw + LANE:(h + 1) * hw] * cos + r[:, rot0 + h * LANE:rot0 + (h + 1) * LANE] * sin
        o_ref[:, h * hw + LANE:(h + 1) * hw] = rope.astype(o_ref.dtype)


def _q_up(h, g, w, cosp, sinp, scale, tm=512):
    T = h.shape[0]
    N = N_MIX_HEADS * 2 * LANE
    return pl.pallas_call(
        functools.partial(_qup_body, scale=scale),
        name="mla_q_up",
        grid=(T // tm,),
        in_specs=[pl.BlockSpec((tm, Q_LORA), lambda i: (i, 0)),
                  pl.BlockSpec((1, Q_LORA), lambda i: (0, 0)),
                  pl.BlockSpec(w.shape, lambda i: (0, 0)),
                  pl.BlockSpec((tm, LANE), lambda i: (i, 0)),
                  pl.BlockSpec((tm, LANE), lambda i: (i, 0))],
        out_specs=pl.BlockSpec((tm, N), lambda i: (i, 0)),
        out_shape=jax.ShapeDtypeStruct((T, N), BF16),
        compiler_params=_params("parallel"),
    )(h, g, w, cosp, sinp)


def _kvup_body(c_ref, kr_ref, g_ref, w_ref, cos_ref, sin_ref, kn_ref, v_ref, krp_ref):
    cn = _rms_norm(c_ref[...], g_ref[...]).astype(BF16)
    r = _dot(cn, w_ref[...])
    kn_ref[...] = r[:, :MIX_WIDTH].astype(kn_ref.dtype)
    v_ref[...] = r[:, MIX_WIDTH:].astype(v_ref.dtype)
    kr = kr_ref[...]
    swapped = pltpu.roll(kr, QK_ROPE, 1)
    krp_ref[...] = (kr * cos_ref[...] + swapped * sin_ref[...]).astype(krp_ref.dtype)


def _kv_up(h, h_tail, g, w, cosp, sinp, tm=512):
    T = h.shape[0]
    kr_blk = MEM_WIDTH // LANE
    return pl.pallas_call(
        _kvup_body,
        name="mla_kv_up",
        grid=(T // tm,),
        in_specs=[pl.BlockSpec((tm, KV_LORA), lambda i: (i, 1)),
                  pl.BlockSpec((tm, LANE), lambda i: (i, kr_blk)),
                  pl.BlockSpec((1, KV_LORA), lambda i: (0, 0)),
                  pl.BlockSpec(w.shape, lambda i: (0, 0)),
                  pl.BlockSpec((tm, LANE), lambda i: (i, 0)),
                  pl.BlockSpec((tm, LANE), lambda i: (i, 0))],
        out_specs=[pl.BlockSpec((tm, MIX_WIDTH), lambda i: (i, 0)),
                   pl.BlockSpec((tm, MIX_WIDTH), lambda i: (i, 0)),
                   pl.BlockSpec((tm, LANE), lambda i: (i, 0))],
        out_shape=[jax.ShapeDtypeStruct((T, MIX_WIDTH), BF16),
                   jax.ShapeDtypeStruct((T, MIX_WIDTH), BF16),
                   jax.ShapeDtypeStruct((T, LANE), BF16)],
        compiler_params=_params("parallel"),
    )(h, h_tail, g, w, cosp, sinp)


def _flash_body(q_ref, kn_ref, kr_ref, v_ref, o_ref, m_ref, l_ref, acc_ref, *, tq, nh):
    qi = pl.program_id(2)
    qw = 2 * LANE

    hs = range(nh)
    hd = [slice(h * HEAD_DIM, (h + 1) * HEAD_DIM) for h in hs]

    def scores(j):
        start = pl.multiple_of(j * tq, tq)
        kr = kr_ref[0, pl.ds(start, tq), :]
        ks = [jnp.concatenate([kn_ref[0, pl.ds(start, tq), hd[h]], kr], axis=1) for h in hs]
        return [_dot_nt(q_ref[0, :, h * qw:(h + 1) * qw], ks[h]) for h in hs]

    def update(js, ss):
        nb = range(len(js))
        starts = [pl.multiple_of(j * tq, tq) for j in js]
        rep = lambda col: jnp.broadcast_to(col, (tq, LANE))
        tile = lambda x: jnp.concatenate([x] * (tq // LANE), axis=1)
        m = [m_ref[h] for h in hs]
        m_new = list(m)
        for b in nb:
            m_new = [jnp.maximum(m_new[h], rep(jnp.max(ss[b][h], -1, keepdims=True))) for h in hs]
        a = [jnp.exp2(m[h] - m_new[h]) for h in hs]
        p = [[jnp.exp2(ss[b][h] - tile(m_new[h])) for h in hs] for b in nb]
        l = [a[h] * l_ref[h] for h in hs]
        acc = [a[h] * acc_ref[h] for h in hs]
        for b in nb:
            l = [l[h] + rep(jnp.sum(p[b][h], -1, keepdims=True)) for h in hs]
        pv = [[_dot(p[b][h].astype(BF16), v_ref[0, pl.ds(starts[b], tq), hd[h]]) for h in hs] for b in nb]
        for b in nb:
            acc = [acc[h] + pv[b][h] for h in hs]
        for h in hs:
            m_ref[h] = m_new[h]
            l_ref[h] = l[h]
            acc_ref[h] = acc[h]
        return l, acc

    m_ref[...] = jnp.full(m_ref.shape, -jnp.inf, F32)
    l_ref[...] = jnp.zeros(l_ref.shape, F32)
    acc_ref[...] = jnp.zeros(acc_ref.shape, F32)

    @pl.loop(0, qi // 2)
    def _(j2):
        update([2 * j2, 2 * j2 + 1], [scores(2 * j2), scores(2 * j2 + 1)])

    @pl.when(qi % 2 == 1)
    def _():
        update([qi - 1], [scores(qi - 1)])

    row = lax.broadcasted_iota(jnp.int32, (tq, tq), 0)
    col = lax.broadcasted_iota(jnp.int32, (tq, tq), 1)
    s = [jnp.where(col <= row, sh, -jnp.inf) for sh in scores(qi)]
    l, acc = update([qi], [s])
    o_ref[0] = jnp.concatenate([acc[h] / l[h] for h in hs], axis=1).astype(o_ref.dtype)


def _flash(q, kn, kr, v, tq=512, nh=2):
    B, S, _ = q.shape
    return pl.pallas_call(
        functools.partial(_flash_body, tq=tq, nh=nh),
        name="mla_flash",
        grid=(B, N_MIX_HEADS // nh, S // tq),
        in_specs=[pl.BlockSpec((1, tq, nh * 2 * LANE), lambda b, h, i: (b, i, h)),
                  pl.BlockSpec((1, S, nh * HEAD_DIM), lambda b, h, i: (b, 0, h)),
                  pl.BlockSpec((1, S, LANE), lambda b, h, i: (b, 0, 0)),
                  pl.BlockSpec((1, S, nh * HEAD_DIM), lambda b, h, i: (b, 0, h))],
        out_specs=pl.BlockSpec((1, tq, nh * HEAD_DIM), lambda b, h, i: (b, i, h)),
        out_shape=jax.ShapeDtypeStruct((B, S, MIX_WIDTH), BF16),
        scratch_shapes=[pltpu.VMEM((nh, tq, LANE), F32), pltpu.VMEM((nh, tq, LANE), F32),
                        pltpu.VMEM((nh, tq, HEAD_DIM), F32)],
        compiler_params=_params("parallel", "parallel", "arbitrary"),
    )(q, kn, kr, v)


def _memattn_body(q_ref, kv_ref, o_ref, *, scale):
    for h in range(N_MEM_HEADS):
        q = (q_ref[0, :, h * HEAD_DIM:(h + 1) * HEAD_DIM] * scale).astype(BF16)
        k = kv_ref[0, :, h * HEAD_DIM:(h + 1) * HEAD_DIM].astype(BF16)
        v = kv_ref[0, :, MEM_WIDTH + h * HEAD_DIM:MEM_WIDTH + (h + 1) * HEAD_DIM].astype(BF16)
        s = _dot_nt(q, k)
        p = jnp.exp(s - jnp.max(s, -1, keepdims=True))
        o = _dot(p.astype(BF16), v) / jnp.sum(p, -1, keepdims=True)
        o_ref[0, :, h * HEAD_DIM:(h + 1) * HEAD_DIM] = o.astype(o_ref.dtype)


def _mem_attention(h3, mem_kv, tm=512):
    B, S, _ = h3.shape
    M = mem_kv.shape[1]
    return pl.pallas_call(
        functools.partial(_memattn_body, scale=HEAD_DIM ** -0.5),
        name="mem_attention",
        grid=(B, S // tm),
        in_specs=[pl.BlockSpec((1, tm, MEM_WIDTH), lambda b, i: (b, i, 0)),
                  pl.BlockSpec((1, M, 2 * MEM_WIDTH), lambda b, i: (b, 0, 0))],
        out_specs=pl.BlockSpec((1, tm, MEM_WIDTH), lambda b, i: (b, i, 0)),
        out_shape=jax.ShapeDtypeStruct((B, S, MEM_WIDTH), BF16),
        compiler_params=_params("parallel", "parallel"),
    )(h3, mem_kv)


def _gate_body(ab_ref, alog_ref, dt_ref, kind_ref, o_ref):
    ab = ab_ref[...]
    kind = kind_ref[...]
    g = -jnp.exp(alog_ref[...]) * jax.nn.softplus(ab + dt_ref[...])
    o_ref[...] = jnp.where(kind == 1.0, g, jnp.where(kind == 2.0, jax.nn.sigmoid(ab), 0.0))


def _gdn_gates(h_tail, lane_params, tm=1024):
    T = h_tail.shape[0]
    W = MEM_WIDTH
    row = pl.BlockSpec((1, W), lambda i: (0, 0))
    return pl.pallas_call(
        _gate_body,
        name="gdn_gates",
        grid=(T // tm,),
        in_specs=[pl.BlockSpec((tm, W), lambda i: (i, 1)), row, row, row],
        out_specs=pl.BlockSpec((tm, W), lambda i: (i, 0)),
        out_shape=jax.ShapeDtypeStruct((T, W), F32),
        compiler_params=_params("parallel"),
    )(h_tail, *lane_params)


def _unit_lower_inverses(Ls, masks):
    eye, neg_strict16, blk32_only, blk64_only = masks
    idx = range(len(Ls))

    def bf(xs):
        return [x.astype(BF16) for x in xs]

    n1 = [L * neg_strict16 for L in Ls]
    nb = bf(n1)
    p = [eye + x for x in n1]
    for _ in range(3):
        nb = bf([_dot(nb[i], nb[i]) for i in idx])
        pb = bf(p)
        p = [p[i] + _dot(pb[i], nb[i]) for i in idx]
    for level in (blk32_only, blk64_only):
        pb = bf(p)
        cb = bf([L * level for L in Ls])
        t = bf([_dot(pb[i], cb[i]) for i in idx])
        p = [p[i] - _dot(t[i], pb[i]) for i in idx]
    return p


def _causal_conv_silu(x_ref, ln, prev, w):
    R = x_ref.shape[1]
    taps = [w[CONV_WIDTH - 1 - d:CONV_WIDTH - d, :] for d in range(CONV_WIDTH)]
    body = x_ref[0, SUBLANE:, ln] * taps[0]
    for d in range(1, CONV_WIDTH):
        body = body + x_ref[0, pl.ds(SUBLANE - d, R - SUBLANE), ln] * taps[d]
    xe = jnp.concatenate([prev, x_ref[0, :SUBLANE, ln]], axis=0)
    head = xe[SUBLANE:] * taps[0]
    for d in range(1, CONV_WIDTH):
        head = head + pltpu.roll(xe, d, 0)[SUBLANE:] * taps[d]
    return _silu(jnp.concatenate([head, body], axis=0))


def _l2_normalize(x):
    return x * lax.rsqrt(jnp.sum(x * x, -1, keepdims=True) + L2_EPS)


def _gdn_body(q_ref, k_ref, v_ref, z_ref, qp_ref, kp_ref, vp_ref, wq_ref, wk_ref, wv_ref,
              gate_ref, onorm_ref, mtril_ref, masks_ref, o_ref, state_ref):
    R = q_ref.shape[1]
    nc = R // CHUNK
    step = pl.program_id(2)

    @pl.when(step == 0)
    def _():
        state_ref[...] = jnp.zeros_like(state_ref)

    not_first = (step > 0).astype(F32)
    row = lax.broadcasted_iota(jnp.int32, (R, R), 0)
    col = lax.broadcasted_iota(jnp.int32, (R, R), 1)

    tril = jnp.logical_and((row >> 6) == (col >> 6), col <= row)
    m_tril = mtril_ref[...]
    masks = tuple(masks_ref[i] for i in range(4))

    def split3(a):
        a1 = a.astype(BF16)
        r1 = a - a1.astype(F32)
        a2 = r1.astype(BF16)
        a3 = (r1 - a2.astype(F32)).astype(BF16)
        return a1, a2, a3

    def lane_bcast(a, lane):
        return jnp.broadcast_to(a[:, lane:lane + 1], (R, HEAD_DIM))

    gates = gate_ref[0]
    gparts = split3(gates)
    gc_all = _dot(m_tril, gparts[0]) + _dot(m_tril, gparts[1]) + _dot(m_tril, gparts[2])
    gc_rows = gc_all.T
    g_end_all = jnp.concatenate(
        [jnp.broadcast_to(gc_all[(c + 1) * CHUNK - 1:(c + 1) * CHUNK, :], (CHUNK, LANE)) for c in range(nc)], axis=0)
    egc_all = jnp.exp(gc_all)
    kdec_all = jnp.exp(g_end_all - gc_all)
    glast_all = jnp.exp(g_end_all)

    heads = range(GDN_HEADS)
    lns = [slice(hh * HEAD_DIM, (hh + 1) * HEAD_DIM) for hh in heads]
    q, k, v, beta, egc, decay, kb, kbf, lmat = [], [], [], [], [], [], [], [], []
    for hh, ln in zip(heads, lns):
        xq = _causal_conv_silu(q_ref, ln, qp_ref[0, :, ln] * not_first, wq_ref[:, ln])
        xk = _causal_conv_silu(k_ref, ln, kp_ref[0, :, ln] * not_first, wk_ref[:, ln])
        v.append(_causal_conv_silu(v_ref, ln, vp_ref[0, :, ln] * not_first, wv_ref[:, ln]))
        q.append(_l2_normalize(xq) * (HEAD_DIM ** -0.5))
        k.append(_l2_normalize(xk))
        gc = lane_bcast(gc_all, hh)
        beta.append(lane_bcast(gates, GDN_HEADS + hh))
        egc.append(lane_bcast(egc_all, hh))
        diff = jnp.concatenate([gc] * (R // HEAD_DIM), axis=1) - gc_rows[hh:hh + 1, :]
        decay.append(jnp.exp(jnp.where(tril, diff, MASKED_LOG)))
        kb.append(k[hh] * beta[hh])
        kbf.append(k[hh].astype(BF16))
        lmat.append(_dot_nt(kb[hh].astype(BF16), kbf[hh]) * decay[hh])

    tinv = _unit_lower_inverses(lmat, masks)

    u, w, a_qk, q_dec, k_dec = [], [], [], [], []
    for hh in heads:
        rhs = jnp.concatenate([v[hh] * beta[hh], kb[hh] * egc[hh]], axis=1).astype(BF16)
        sol = _dot(tinv[hh].astype(BF16), rhs)
        u.append(sol[:, :HEAD_DIM])
        w.append(sol[:, HEAD_DIM:].astype(BF16))
        a_qk.append((_dot_nt(q[hh].astype(BF16), kbf[hh]) * decay[hh]).astype(BF16))
        q_dec.append((q[hh] * egc[hh]).astype(BF16))
        k_dec.append((k[hh] * lane_bcast(kdec_all, hh)).astype(BF16))

    state = [state_ref[hh] for hh in heads]
    v_new = [[] for _ in heads]
    o_inter = [[] for _ in heads]
    for c in range(nc):
        rows = slice(c * CHUNK, (c + 1) * CHUNK)
        for hh in heads:
            both = _dot(jnp.concatenate([w[hh][rows], q_dec[hh][rows]], axis=0), state[hh].astype(BF16))
            vn = u[hh][rows] - both[:CHUNK]
            o_inter[hh].append(both[CHUNK:])
            g_last = glast_all[c * CHUNK:c * CHUNK + 1, hh:hh + 1]
            state[hh] = state[hh] * g_last + _dot_tn(k_dec[hh][rows], vn.astype(BF16))
            v_new[hh].append(vn)
    state_ref[...] = jnp.stack(state)

    outs = []
    for hh, ln in zip(heads, lns):
        vn_all = jnp.concatenate(v_new[hh], axis=0).astype(BF16)
        o = jnp.concatenate(o_inter[hh], axis=0) + _dot(a_qk[hh], vn_all)
        outs.append(_rms_norm(o, onorm_ref[...]) * _silu(z_ref[0, :, ln]))
    o_ref[0] = jnp.concatenate(outs, axis=1).astype(o_ref.dtype)


def _gdn_block_masks(R):
    row = lax.broadcasted_iota(jnp.int32, (R, R), 0)
    col = lax.broadcasted_iota(jnp.int32, (R, R), 1)

    def same(n):
        return (row // n) == (col // n)

    one = lambda c: jnp.where(c, 1.0, 0.0).astype(F32)
    tril = one(jnp.logical_and(same(CHUNK), col <= row)).astype(BF16)
    eye = one(row == col)
    neg_strict16 = -one(jnp.logical_and(same(16), col < row))
    blk32_only = one(jnp.logical_and(same(32), jnp.logical_not(same(16))))
    blk64_only = one(jnp.logical_and(same(CHUNK), jnp.logical_not(same(32))))
    return tril, jnp.stack([eye, neg_strict16, blk32_only, blk64_only])


def _gdn_mix(h3, conv_w, gates, o_norm):
    B, S, _ = h3.shape
    R, G = GDN_ROWS, GDN_HEADS
    W = G * HEAD_DIM
    ng = N_MIX_HEADS // G
    rb = R // SUBLANE

    def cur(off):
        return pl.BlockSpec((1, R, W), lambda b, g, i: (b, i, off * ng + g))

    def prev(off):
        return pl.BlockSpec((1, SUBLANE, W), lambda b, g, i: (b, jnp.maximum(i * rb - 1, 0), off * ng + g))

    def convw(off):
        return pl.BlockSpec((CONV_WIDTH, W), lambda b, g, i: (0, off * ng + g))

    return pl.pallas_call(
        _gdn_body,
        name="gdn_delta_rule",
        grid=(B, ng, S // R),
        in_specs=[cur(0), cur(1), cur(2), cur(3), prev(0), prev(1), prev(2),
                  convw(0), convw(1), convw(2),
                  pl.BlockSpec((1, R, LANE), lambda b, g, i: (b, i, g)),
                  pl.BlockSpec((1, HEAD_DIM), lambda b, g, i: (0, 0)),
                  pl.BlockSpec((R, R), lambda b, g, i: (0, 0)),
                  pl.BlockSpec((4, R, R), lambda b, g, i: (0, 0, 0))],
        out_specs=pl.BlockSpec((1, R, W), lambda b, g, i: (b, i, g)),
        out_shape=jax.ShapeDtypeStruct((B, S, MIX_WIDTH), BF16),
        scratch_shapes=[pltpu.VMEM((G, HEAD_DIM, HEAD_DIM), F32)],
        compiler_params=_params("parallel", "parallel", "arbitrary"),
    )(h3, h3, h3, h3, h3, h3, h3, conv_w, conv_w, conv_w, gates, o_norm, *_gdn_block_masks(R))


def _rotate_half_cols(w):
    half = w.shape[-1] // 2
    return jnp.concatenate([-w[..., half:], w[..., :half]], -1)


def _mla_weights(w_in, w_uq, w_ukv):
    k_r = w_in[:, Q_LORA + KV_LORA:Q_LORA + KV_LORA + QK_ROPE]
    q_mem = w_in[:, Q_LORA + KV_LORA + QK_ROPE:]
    w_tail = jnp.concatenate([q_mem, k_r, _rotate_half_cols(k_r)], 1).astype(BF16)

    uq = w_uq.reshape(Q_LORA, N_MIX_HEADS, QK_NOPE + QK_ROPE)
    nope, rope = uq[..., :QK_NOPE], uq[..., QK_NOPE:]
    zeros = jnp.zeros_like(rope)
    main = jnp.concatenate([nope, rope, zeros], -1).reshape(Q_LORA, -1)
    rot = jnp.concatenate([_rotate_half_cols(rope), zeros], -1).reshape(Q_LORA, -1)
    w_uq_p = jnp.concatenate([main, rot], 1).astype(BF16)

    ukv = w_ukv.reshape(KV_LORA, N_MIX_HEADS, QK_NOPE + HEAD_DIM)
    w_ukv_p = jnp.concatenate([ukv[..., :QK_NOPE].reshape(KV_LORA, -1),
                               ukv[..., QK_NOPE:].reshape(KV_LORA, -1)], 1).astype(BF16)
    return w_tail, w_uq_p, w_ukv_p


def _group_lanes(a, b):
    G = GDN_HEADS
    lead = a.shape[:-1]
    a = a.reshape(lead + (N_MIX_HEADS // G, G))
    b = b.reshape(lead + (N_MIX_HEADS // G, G))
    pad = jnp.zeros(lead + (N_MIX_HEADS // G, LANE - 2 * G), a.dtype)
    return jnp.concatenate([a, b, pad], -1).reshape(lead + (-1,))


def _gdn_in_weights(w_in, a_log, dt_bias):
    D = w_in.shape[0]
    a = w_in[:, 4 * MIX_WIDTH:4 * MIX_WIDTH + N_MIX_HEADS]
    b = w_in[:, 4 * MIX_WIDTH + N_MIX_HEADS:4 * MIX_WIDTH + 2 * N_MIX_HEADS]
    q_mem = w_in[:, 4 * MIX_WIDTH + 2 * N_MIX_HEADS:]
    gate_w = _group_lanes(a, b)
    pad = jnp.zeros((D, MEM_WIDTH - gate_w.shape[1]), w_in.dtype)
    w_tail = jnp.concatenate([q_mem, gate_w, pad], 1).astype(BF16)

    def lanes(va, vb):
        v = _group_lanes(va, vb)
        return jnp.pad(v, (0, MEM_WIDTH - v.shape[0]))[None, :]

    zero, one = jnp.zeros_like(a_log), jnp.ones_like(a_log)
    return w_tail, (lanes(a_log, zero), lanes(dt_bias, zero), lanes(one, 2.0 * one))


def _pad_lanes(v):
    return jnp.pad(v, (0, LANE - v.shape[0]))[None, :]


def kernel(x, mem, positions, mla_w_in, mla_q_norm, mla_w_uq, mla_kv_norm, mla_w_ukv, gdn_w_in, gdn_conv, gdn_a_log, gdn_dt_bias, gdn_o_norm, mem_w_kv, w_out, ln1_g, ln1_b, mlp_w1, mlp_w2, ln2_g, ln2_b):
    B, S, D = x.shape
    T = B * S
    depth = w_out.shape[0]
    alpha = (2 * depth) ** 0.25

    inv_freq = 1.0 / (ROPE_THETA ** (jnp.arange(0, QK_ROPE, 2, dtype=F32) / QK_ROPE))
    freq = _pad_lanes(jnp.concatenate([inv_freq, inv_freq]))
    cosp, sinp = _rope_tables(positions.reshape(T, 1), freq)

    xf = x.reshape(T, D)
    xb = xf
    mem2 = mem.reshape(B * mem.shape[1], D)
    mem_w_kv_b, w_out_b = mem_w_kv.astype(BF16), w_out.astype(BF16)
    mlp_w1_b, mlp_w2_b = mlp_w1.astype(BF16), mlp_w2.astype(BF16)
    mla_w_in_b, gdn_w_in_b = mla_w_in.astype(BF16), gdn_w_in.astype(BF16)
    for i in range(depth):
        j = i // 2
        mem_kv = _matmul(mem2, mem_w_kv_b, F32, tm=256, tn=512, layer=i, n_cols=2 * MEM_WIDTH)
        mem_kv = mem_kv.reshape(B, mem.shape[1], 2 * MEM_WIDTH)
        if i % 2 == 0:
            w_tail, w_uq_p, w_ukv_p = _mla_weights(mla_w_in[j], mla_w_uq[j], mla_w_ukv[j])
            h = _matmul(xb, mla_w_in_b, F32, tm=1024, tn=Q_LORA + KV_LORA, layer=j, n_cols=Q_LORA + KV_LORA)
            h_tail = _matmul(xb, w_tail, F32, tm=1024, tn=w_tail.shape[1])
            q = _q_up(h, mla_q_norm[j][None, :], w_uq_p, cosp, sinp, (QK_NOPE + QK_ROPE) ** -0.5 * LOG2_E)
            kn, v, krp = _kv_up(h, h_tail, mla_kv_norm[j][None, :], w_ukv_p, cosp, sinp)
            mix = _flash(q.reshape(B, S, -1), kn.reshape(B, S, -1), krp.reshape(B, S, -1),
                         v.reshape(B, S, -1)).reshape(T, MIX_WIDTH)
        else:
            w_tail, lane_params = _gdn_in_weights(gdn_w_in[j], gdn_a_log[j], gdn_dt_bias[j])
            h = _matmul(xb, gdn_w_in_b, F32, tm=1024, tn=MIX_WIDTH, layer=j, n_cols=4 * MIX_WIDTH)
            h_tail = _matmul(xb, w_tail, F32, tm=1024, tn=w_tail.shape[1])
            gates = _gdn_gates(h_tail, lane_params)
            mix = _gdn_mix(h.reshape(B, S, -1), gdn_conv[j], gates.reshape(B, S, -1),
                           gdn_o_norm[j][None, :]).reshape(T, MIX_WIDTH)
        mem_o = _mem_attention(h_tail.reshape(B, S, -1), mem_kv)
        xf, xb = _outproj_ln(mix, mem_o.reshape(T, MEM_WIDTH), w_out_b, i, xf,
                             ln1_g[i][None, :], ln1_b[i][None, :], alpha)
        xf, xb = _mlp_ln(xf, xb, mlp_w1_b, mlp_w2_b, i, ln2_g[i][None, :], ln2_b[i][None, :], alpha)
    return xf.reshape(B, S, D)
```

```python
import functools

import jax
import jax.numpy as jnp
from jax import lax
from jax.experimental import pallas as pl
from jax.experimental.pallas import tpu as pltpu

F32 = jnp.float32
BF16 = jnp.bfloat16

HEAD_DIM = 128
N_MIX_HEADS = 12
N_MEM_HEADS = 4
MIX_WIDTH = N_MIX_HEADS * HEAD_DIM
MEM_WIDTH = N_MEM_HEADS * HEAD_DIM
Q_LORA = 512
KV_LORA = 512
QK_NOPE = 128
QK_ROPE = 64
ROPE_THETA = 10000.0
CONV_WIDTH = 4
CHUNK = 64
LN_EPS = 1e-5
RMS_EPS = 1e-6
L2_EPS = 1e-6

LANE = 128
SUBLANE = 8
VMEM_LIMIT = 56 * 1024 * 1024

GDN_ROWS = 128
GDN_HEADS = 12
LOG2_E = 1.4426950408889634
MASKED_LOG = -1e30


def _params(*sem):
    return pltpu.CompilerParams(dimension_semantics=sem, vmem_limit_bytes=VMEM_LIMIT)


def _dot(a, b):
    return jnp.dot(a, b, preferred_element_type=F32)


def _dot_nt(a, b):
    return lax.dot_general(a, b, (((1,), (1,)), ((), ())), preferred_element_type=F32)


def _dot_tn(a, b):
    return lax.dot_general(a, b, (((0,), (0,)), ((), ())), preferred_element_type=F32)


def _layer_norm(r, g, b):
    mu = jnp.mean(r, -1, keepdims=True)
    d = r - mu
    var = jnp.mean(d * d, -1, keepdims=True)
    return d * lax.rsqrt(var + LN_EPS) * g + b


def _rms_norm(x, g):
    return x * lax.rsqrt(jnp.mean(x * x, -1, keepdims=True) + RMS_EPS) * g


def _silu(x):
    return x / (1.0 + jnp.exp(-x))


def _mm_body(x_ref, w_ref, o_ref):
    o_ref[...] = _dot(x_ref[...].astype(BF16), w_ref[...]).astype(o_ref.dtype)


def _matmul(x, w, out_dtype, tm, tn, layer=None, n_cols=None):
    M, K = x.shape
    if layer is None:
        N = w.shape[1]
        w_spec = pl.BlockSpec((K, tn), lambda i, j: (0, j))
    else:
        N = n_cols
        w_spec = pl.BlockSpec((None, K, tn), lambda i, j: (layer, 0, j))
    return pl.pallas_call(
        _mm_body,
        name="matmul",
        grid=(M // tm, N // tn),
        in_specs=[pl.BlockSpec((tm, K), lambda i, j: (i, 0)), w_spec],
        out_specs=pl.BlockSpec((tm, tn), lambda i, j: (i, j)),
        out_shape=jax.ShapeDtypeStruct((M, N), out_dtype),
        compiler_params=_params("parallel", "parallel"),
    )(x, w)


def _outproj_body(mix_ref, mem_ref, wa_ref, wb_ref, x_ref, g_ref, b_ref, o_ref, ob_ref, *, alpha):
    y = _dot(mix_ref[...], wa_ref[...]) + _dot(mem_ref[...], wb_ref[...])
    o = _layer_norm(alpha * x_ref[...] + y, g_ref[...], b_ref[...])
    o_ref[...] = o
    ob_ref[...] = o.astype(BF16)


def _outproj_ln(mix, memo, w_all, layer, x, g, b, alpha, tm=512):
    M, D = x.shape
    mem_row_blk = mix.shape[1] // memo.shape[1]
    return pl.pallas_call(
        functools.partial(_outproj_body, alpha=alpha),
        name="outproj_ln",
        grid=(M // tm,),
        in_specs=[pl.BlockSpec((tm, mix.shape[1]), lambda i: (i, 0)),
                  pl.BlockSpec((tm, memo.shape[1]), lambda i: (i, 0)),
                  pl.BlockSpec((None, mix.shape[1], D), lambda i: (layer, 0, 0)),
                  pl.BlockSpec((None, memo.shape[1], D), lambda i: (layer, mem_row_blk, 0)),
                  pl.BlockSpec((tm, D), lambda i: (i, 0)),
                  pl.BlockSpec((1, D), lambda i: (0, 0)),
                  pl.BlockSpec((1, D), lambda i: (0, 0))],
        out_specs=[pl.BlockSpec((tm, D), lambda i: (i, 0))] * 2,
        out_shape=[jax.ShapeDtypeStruct((M, D), F32), jax.ShapeDtypeStruct((M, D), BF16)],
        compiler_params=_params("parallel"),
    )(mix, memo, w_all, w_all, x, g, b)


def _mlp_body(x_ref, xb_ref, w1_ref, w2_ref, g_ref, b_ref, o_ref, ob_ref, acc_ref, *, alpha):
    f = pl.program_id(1)

    @pl.when(f == 0)
    def _():
        acc_ref[...] = jnp.zeros_like(acc_ref)

    h = jnp.maximum(_dot(xb_ref[...], w1_ref[...]), 0.0)
    acc_ref[...] += _dot((h * h).astype(BF16), w2_ref[...])

    @pl.when(f == pl.num_programs(1) - 1)
    def _():
        o = _layer_norm(alpha * x_ref[...] + acc_ref[...], g_ref[...], b_ref[...])
        o_ref[...] = o
        ob_ref[...] = o.astype(BF16)


def _mlp_ln(x, xb, w1, w2, layer, g, b, alpha, tm=512, tf=1024):
    M, D = x.shape
    FF = w1.shape[2]
    return pl.pallas_call(
        functools.partial(_mlp_body, alpha=alpha),
        name="mlp_ln",
        grid=(M // tm, FF // tf),
        in_specs=[pl.BlockSpec((tm, D), lambda i, f: (i, 0)),
                  pl.BlockSpec((tm, D), lambda i, f: (i, 0)),
                  pl.BlockSpec((None, D, tf), lambda i, f: (layer, 0, f)),
                  pl.BlockSpec((None, tf, D), lambda i, f: (layer, f, 0)),
                  pl.BlockSpec((1, D), lambda i, f: (0, 0)),
                  pl.BlockSpec((1, D), lambda i, f: (0, 0))],
        out_specs=[pl.BlockSpec((tm, D), lambda i, f: (i, 0))] * 2,
        out_shape=[jax.ShapeDtypeStruct((M, D), F32), jax.ShapeDtypeStruct((M, D), BF16)],
        scratch_shapes=[pltpu.VMEM((tm, D), F32)],
        compiler_params=_params("parallel", "arbitrary"),
    )(x, xb, w1, w2, g, b)


def _rope_body(pos_ref, freq_ref, cos_ref, sin_ref):
    ang = pos_ref[...].astype(F32) * freq_ref[...]
    lane = lax.broadcasted_iota(jnp.int32, ang.shape, 1)
    keep = lane < QK_ROPE
    cos_ref[...] = jnp.where(keep, jnp.cos(ang), 0.0)
    sin_ref[...] = jnp.where(keep, jnp.sin(ang), 0.0)


def _rope_tables(pos, freq, tm=1024):
    T = pos.shape[0]
    shp = jax.ShapeDtypeStruct((T, LANE), F32)
    return pl.pallas_call(
        _rope_body,
        name="rope_tables",
        grid=(T // tm,),
        in_specs=[pl.BlockSpec((tm, 1), lambda i: (i, 0)),
                  pl.BlockSpec((1, LANE), lambda i: (0, 0))],
        out_specs=[pl.BlockSpec((tm, LANE), lambda i: (i, 0))] * 2,
        out_shape=[shp, shp],
        compiler_params=_params("parallel"),
    )(pos, freq)


def _qup_body(c_ref, g_ref, w_ref, cos_ref, sin_ref, o_ref, *, scale):
    cn = _rms_norm(c_ref[...], g_ref[...]).astype(BF16)
    r = _dot(cn, w_ref[...])
    cos = cos_ref[...] * scale
    sin = sin_ref[...] * scale
    hw = 2 * LANE
    rot0 = N_MIX_HEADS * hw
    for h in range(N_MIX_HEADS):
        o_ref[:, h * hw:h * hw + LANE] = (r[:, h * hw:h * hw + LANE] * scale).astype(o_ref.dtype)
        rope = r[:, h * hw + LANE:(h + 1) * hw] * cos + r[:, rot0 + h * LANE:rot0 + (h + 1) * LANE] * sin
        o_ref[:, h * hw + LANE:(h + 1) * hw] = rope.astype(o_ref.dtype)


def _q_up(h, g, w, cosp, sinp, scale, tm=512):
    T = h.shape[0]
    N = N_MIX_HEADS * 2 * LANE
    return pl.pallas_call(
        functools.partial(_qup_body, scale=scale),
        name="mla_q_up",
        grid=(T // tm,),
        in_specs=[pl.BlockSpec((tm, Q_LORA), lambda i: (i, 0)),
                  pl.BlockSpec((1, Q_LORA), lambda i: (0, 0)),
                  pl.BlockSpec(w.shape, lambda i: (0, 0)),
                  pl.BlockSpec((tm, LANE), lambda i: (i, 0)),
                  pl.BlockSpec((tm, LANE), lambda i: (i, 0))],
        out_specs=pl.BlockSpec((tm, N), lambda i: (i, 0)),
        out_shape=jax.ShapeDtypeStruct((T, N), BF16),
        compiler_params=_params("parallel"),
    )(h, g, w, cosp, sinp)


def _kvup_body(c_ref, kr_ref, g_ref, w_ref, cos_ref, sin_ref, kn_ref, v_ref, krp_ref):
    cn = _rms_norm(c_ref[...], g_ref[...]).astype(BF16)
    r = _dot(cn, w_ref[...])
    kn_ref[...] = r[:, :MIX_WIDTH].astype(kn_ref.dtype)
    v_ref[...] = r[:, MIX_WIDTH:].astype(v_ref.dtype)
    kr = kr_ref[...]
    swapped = pltpu.roll(kr, QK_ROPE, 1)
    krp_ref[...] = (kr * cos_ref[...] + swapped * sin_ref[...]).astype(krp_ref.dtype)


def _kv_up(h, h_tail, g, w, cosp, sinp, tm=512):
    T = h.shape[0]
    kr_blk = MEM_WIDTH // LANE
    return pl.pallas_call(
        _kvup_body,
        name="mla_kv_up",
        grid=(T // tm,),
        in_specs=[pl.BlockSpec((tm, KV_LORA), lambda i: (i, 1)),
                  pl.BlockSpec((tm, LANE), lambda i: (i, kr_blk)),
                  pl.BlockSpec((1, KV_LORA), lambda i: (0, 0)),
                  pl.BlockSpec(w.shape, lambda i: (0, 0)),
                  pl.BlockSpec((tm, LANE), lambda i: (i, 0)),
                  pl.BlockSpec((tm, LANE), lambda i: (i, 0))],
        out_specs=[pl.BlockSpec((tm, MIX_WIDTH), lambda i: (i, 0)),
                   pl.BlockSpec((tm, MIX_WIDTH), lambda i: (i, 0)),
                   pl.BlockSpec((tm, LANE), lambda i: (i, 0))],
        out_shape=[jax.ShapeDtypeStruct((T, MIX_WIDTH), BF16),
                   jax.ShapeDtypeStruct((T, MIX_WIDTH), BF16),
                   jax.ShapeDtypeStruct((T, LANE), BF16)],
        compiler_params=_params("parallel"),
    )(h, h_tail, g, w, cosp, sinp)


def _flash_body(q_ref, kn_ref, kr_ref, v_ref, o_ref, m_ref, l_ref, acc_ref, *, tq, nh):
    qi = pl.program_id(2)
    qw = 2 * LANE

    hs = range(nh)
    hd = [slice(h * HEAD_DIM, (h + 1) * HEAD_DIM) for h in hs]

    def scores(j):
        start = pl.multiple_of(j * tq, tq)
        kr = kr_ref[0, pl.ds(start, tq), :]
        ks = [jnp.concatenate([kn_ref[0, pl.ds(start, tq), hd[h]], kr], axis=1) for h in hs]
        return [_dot_nt(q_ref[0, :, h * qw:(h + 1) * qw], ks[h]) for h in hs]

    def update(js, ss):
        nb = range(len(js))
        starts = [pl.multiple_of(j * tq, tq) for j in js]
        rep = lambda col: jnp.broadcast_to(col, (tq, LANE))
        tile = lambda x: jnp.concatenate([x] * (tq // LANE), axis=1)
        m = [m_ref[h] for h in hs]
        m_new = list(m)
        for b in nb:
            m_new = [jnp.maximum(m_new[h], rep(jnp.max(ss[b][h], -1, keepdims=True))) for h in hs]
        a = [jnp.exp2(m[h] - m_new[h]) for h in hs]
        p = [[jnp.exp2(ss[b][h] - tile(m_new[h])) for h in hs] for b in nb]
        l = [a[h] * l_ref[h] for h in hs]
        acc = [a[h] * acc_ref[h] for h in hs]
        for b in nb:
            l = [l[h] + rep(jnp.sum(p[b][h], -1, keepdims=True)) for h in hs]
        pv = [[_dot(p[b][h].astype(BF16), v_ref[0, pl.ds(starts[b], tq), hd[h]]) for h in hs] for b in nb]
        for b in nb:
            acc = [acc[h] + pv[b][h] for h in hs]
        for h in hs:
            m_ref[h] = m_new[h]
            l_ref[h] = l[h]
            acc_ref[h] = acc[h]
        return l, acc

    m_ref[...] = jnp.full(m_ref.shape, -jnp.inf, F32)
    l_ref[...] = jnp.zeros(l_ref.shape, F32)
    acc_ref[...] = jnp.zeros(acc_ref.shape, F32)

    @pl.loop(0, qi // 2)
    def _(j2):
        update([2 * j2, 2 * j2 + 1], [scores(2 * j2), scores(2 * j2 + 1)])

    @pl.when(qi % 2 == 1)
    def _():
        update([qi - 1], [scores(qi - 1)])

    row = lax.broadcasted_iota(jnp.int32, (tq, tq), 0)
    col = lax.broadcasted_iota(jnp.int32, (tq, tq), 1)
    s = [jnp.where(col <= row, sh, -jnp.inf) for sh in scores(qi)]
    l, acc = update([qi], [s])
    o_ref[0] = jnp.concatenate([acc[h] / l[h] for h in hs], axis=1).astype(o_ref.dtype)


def _flash(q, kn, kr, v, tq=512, nh=2):
    B, S, _ = q.shape
    return pl.pallas_call(
        functools.partial(_flash_body, tq=tq, nh=nh),
        name="mla_flash",
        grid=(B, N_MIX_HEADS // nh, S // tq),
        in_specs=[pl.BlockSpec((1, tq, nh * 2 * LANE), lambda b, h, i: (b, i, h)),
                  pl.BlockSpec((1, S, nh * HEAD_DIM), lambda b, h, i: (b, 0, h)),
                  pl.BlockSpec((1, S, LANE), lambda b, h, i: (b, 0, 0)),
                  pl.BlockSpec((1, S, nh * HEAD_DIM), lambda b, h, i: (b, 0, h))],
        out_specs=pl.BlockSpec((1, tq, nh * HEAD_DIM), lambda b, h, i: (b, i, h)),
        out_shape=jax.ShapeDtypeStruct((B, S, MIX_WIDTH), BF16),
        scratch_shapes=[pltpu.VMEM((nh, tq, LANE), F32), pltpu.VMEM((nh, tq, LANE), F32),
                        pltpu.VMEM((nh, tq, HEAD_DIM), F32)],
        compiler_params=_params("parallel", "parallel", "arbitrary"),
    )(q, kn, kr, v)


def _memattn_body(q_ref, kv_ref, o_ref, *, scale):
    for h in range(N_MEM_HEADS):
        q = (q_ref[0, :, h * HEAD_DIM:(h + 1) * HEAD_DIM] * scale).astype(BF16)
        k = kv_ref[0, :, h * HEAD_DIM:(h + 1) * HEAD_DIM].astype(BF16)
        v = kv_ref[0, :, MEM_WIDTH + h * HEAD_DIM:MEM_WIDTH + (h + 1) * HEAD_DIM].astype(BF16)
        s = _dot_nt(q, k)
        p = jnp.exp(s - jnp.max(s, -1, keepdims=True))
        o = _dot(p.astype(BF16), v) / jnp.sum(p, -1, keepdims=True)
        o_ref[0, :, h * HEAD_DIM:(h + 1) * HEAD_DIM] = o.astype(o_ref.dtype)


def _mem_attention(h3, mem_kv, tm=512):
    B, S, _ = h3.shape
    M = mem_kv.shape[1]
    return pl.pallas_call(
        functools.partial(_memattn_body, scale=HEAD_DIM ** -0.5),
        name="mem_attention",
        grid=(B, S // tm),
        in_specs=[pl.BlockSpec((1, tm, MEM_WIDTH), lambda b, i: (b, i, 0)),
                  pl.BlockSpec((1, M, 2 * MEM_WIDTH), lambda b, i: (b, 0, 0))],
        out_specs=pl.BlockSpec((1, tm, MEM_WIDTH), lambda b, i: (b, i, 0)),
        out_shape=jax.ShapeDtypeStruct((B, S, MEM_WIDTH), BF16),
        compiler_params=_params("parallel", "parallel"),
    )(h3, mem_kv)


def _gate_body(ab_ref, alog_ref, dt_ref, kind_ref, o_ref):
    ab = ab_ref[...]
    kind = kind_ref[...]
    g = -jnp.exp(alog_ref[...]) * jax.nn.softplus(ab + dt_ref[...])
    o_ref[...] = jnp.where(kind == 1.0, g, jnp.where(kind == 2.0, jax.nn.sigmoid(ab), 0.0))


def _gdn_gates(h_tail, lane_params, tm=1024):
    T = h_tail.shape[0]
    W = MEM_WIDTH
    row = pl.BlockSpec((1, W), lambda i: (0, 0))
    return pl.pallas_call(
        _gate_body,
        name="gdn_gates",
        grid=(T // tm,),
        in_specs=[pl.BlockSpec((tm, W), lambda i: (i, 1)), row, row, row],
        out_specs=pl.BlockSpec((tm, W), lambda i: (i, 0)),
        out_shape=jax.ShapeDtypeStruct((T, W), F32),
        compiler_params=_params("parallel"),
    )(h_tail, *lane_params)


def _unit_lower_inverses(Ls, masks):
    eye, neg_strict16, blk32_only, blk64_only = masks
    idx = range(len(Ls))

    def bf(xs):
        return [x.astype(BF16) for x in xs]

    n1 = [L * neg_strict16 for L in Ls]
    nb = bf(n1)
    p = [eye + x for x in n1]
    for _ in range(3):
        nb = bf([_dot(nb[i], nb[i]) for i in idx])
        pb = bf(p)
        p = [p[i] + _dot(pb[i], nb[i]) for i in idx]
    for level in (blk32_only, blk64_only):
        pb = bf(p)
        cb = bf([L * level for L in Ls])
        t = bf([_dot(pb[i], cb[i]) for i in idx])
        p = [p[i] - _dot(t[i], pb[i]) for i in idx]
    return p


def _causal_conv_silu(x_ref, ln, prev, w):
    assert CONV_WIDTH == 4
    t0, t1, t2, t3 = (w[CONV_WIDTH - 1 - d:CONV_WIDTH - d, :] for d in range(CONV_WIDTH))
    xe = jnp.concatenate([prev, x_ref[0, :, ln]], axis=0)
    xs = pltpu.roll(xe, 1, 0)
    y = (xe * t0 + xs * t1) + pltpu.roll(xe * t2 + xs * t3, 2, 0)
    return _silu(y[SUBLANE:])


def _l2_normalize(x):
    return x * lax.rsqrt(jnp.sum(x * x, -1, keepdims=True) + L2_EPS)


def _gdn_body(q_ref, k_ref, v_ref, z_ref, qp_ref, kp_ref, vp_ref, wq_ref, wk_ref, wv_ref,
              gate_ref, onorm_ref, mtril_ref, masks_ref, o_ref, state_ref):
    R = q_ref.shape[1]
    nc = R // CHUNK
    step = pl.program_id(2)

    @pl.when(step == 0)
    def _():
        state_ref[...] = jnp.zeros_like(state_ref)

    not_first = (step > 0).astype(F32)
    row = lax.broadcasted_iota(jnp.int32, (R, R), 0)
    col = lax.broadcasted_iota(jnp.int32, (R, R), 1)

    tril = jnp.logical_and((row >> 6) == (col >> 6), col <= row)
    m_tril = mtril_ref[...]
    masks = tuple(masks_ref[i] for i in range(4))

    def split3(a):
        a1 = a.astype(BF16)
        r1 = a - a1.astype(F32)
        a2 = r1.astype(BF16)
        a3 = (r1 - a2.astype(F32)).astype(BF16)
        return a1, a2, a3

    def lane_bcast(a, lane):
        return jnp.broadcast_to(a[:, lane:lane + 1], (R, HEAD_DIM))

    gates = gate_ref[0]
    gparts = split3(gates)
    gc_all = _dot(m_tril, gparts[0]) + _dot(m_tril, gparts[1]) + _dot(m_tril, gparts[2])
    gc_rows = gc_all.T
    g_end_all = jnp.concatenate(
        [jnp.broadcast_to(gc_all[(c + 1) * CHUNK - 1:(c + 1) * CHUNK, :], (CHUNK, LANE)) for c in range(nc)], axis=0)
    egc_all = jnp.exp(gc_all)
    kdec_all = jnp.exp(g_end_all - gc_all)
    glast_all = jnp.exp(g_end_all)

    heads = range(GDN_HEADS)
    lns = [slice(hh * HEAD_DIM, (hh + 1) * HEAD_DIM) for hh in heads]
    q, k, v, beta, egc, decay, kb, kbf, lmat = [], [], [], [], [], [], [], [], []
    for hh, ln in zip(heads, lns):
        xq = _causal_conv_silu(q_ref, ln, qp_ref[0, :, ln] * not_first, wq_ref[:, ln])
        xk = _causal_conv_silu(k_ref, ln, kp_ref[0, :, ln] * not_first, wk_ref[:, ln])
        v.append(_causal_conv_silu(v_ref, ln, vp_ref[0, :, ln] * not_first, wv_ref[:, ln]))
        q.append(_l2_normalize(xq) * (HEAD_DIM ** -0.5))
        k.append(_l2_normalize(xk))
        gc = lane_bcast(gc_all, hh)
        beta.append(lane_bcast(gates, GDN_HEADS + hh))
        egc.append(lane_bcast(egc_all, hh))
        diff = jnp.concatenate([gc] * (R // HEAD_DIM), axis=1) - gc_rows[hh:hh + 1, :]
        decay.append(jnp.exp(jnp.where(tril, diff, MASKED_LOG)))
        kb.append(k[hh] * beta[hh])
        kbf.append(k[hh].astype(BF16))
        lmat.append(_dot_nt(kb[hh].astype(BF16), kbf[hh]) * decay[hh])

    tinv = _unit_lower_inverses(lmat, masks)

    u, w, a_qk, q_dec, k_dec = [], [], [], [], []
    for hh in heads:
        rhs = jnp.concatenate([v[hh] * beta[hh], kb[hh] * egc[hh]], axis=1).astype(BF16)
        sol = _dot(tinv[hh].astype(BF16), rhs)
        u.append(sol[:, :HEAD_DIM])
        w.append(sol[:, HEAD_DIM:].astype(BF16))
        a_qk.append((_dot_nt(q[hh].astype(BF16), kbf[hh]) * decay[hh]).astype(BF16))
        q_dec.append((q[hh] * egc[hh]).astype(BF16))
        k_dec.append((k[hh] * lane_bcast(kdec_all, hh)).astype(BF16))

    state = [state_ref[hh] for hh in heads]
    v_new = [[] for _ in heads]
    o_inter = [[] for _ in heads]
    for c in range(nc):
        rows = slice(c * CHUNK, (c + 1) * CHUNK)
        for hh in heads:
            both = _dot(jnp.concatenate([w[hh][rows], q_dec[hh][rows]], axis=0), state[hh].astype(BF16))
            vn = u[hh][rows] - both[:CHUNK]
            o_inter[hh].append(both[CHUNK:])
            g_last = glast_all[c * CHUNK:c * CHUNK + 1, hh:hh + 1]
            state[hh] = state[hh] * g_last + _dot_tn(k_dec[hh][rows], vn.astype(BF16))
            v_new[hh].append(vn)
    state_ref[...] = jnp.stack(state)

    outs = []
    for hh, ln in zip(heads, lns):
        vn_all = jnp.concatenate(v_new[hh], axis=0).astype(BF16)
        o = jnp.concatenate(o_inter[hh], axis=0) + _dot(a_qk[hh], vn_all)
        outs.append(_rms_norm(o, onorm_ref[...]) * _silu(z_ref[0, :, ln]))
    o_ref[0] = jnp.concatenate(outs, axis=1).astype(o_ref.dtype)


def _gdn_block_masks(R):
    row = lax.broadcasted_iota(jnp.int32, (R, R), 0)
    col = lax.broadcasted_iota(jnp.int32, (R, R), 1)

    def same(n):
        return (row // n) == (col // n)

    one = lambda c: jnp.where(c, 1.0, 0.0).astype(F32)
    tril = one(jnp.logical_and(same(CHUNK), col <= row)).astype(BF16)
    eye = one(row == col)
    neg_strict16 = -one(jnp.logical_and(same(16), col < row))
    blk32_only = one(jnp.logical_and(same(32), jnp.logical_not(same(16))))
    blk64_only = one(jnp.logical_and(same(CHUNK), jnp.logical_not(same(32))))
    return tril, jnp.stack([eye, neg_strict16, blk32_only, blk64_only])


def _gdn_mix(h3, conv_w, gates, o_norm):
    B, S, _ = h3.shape
    R, G = GDN_ROWS, GDN_HEADS
    W = G * HEAD_DIM
    ng = N_MIX_HEADS // G
    rb = R // SUBLANE

    def cur(off):
        return pl.BlockSpec((1, R, W), lambda b, g, i: (b, i, off * ng + g))

    def prev(off):
        return pl.BlockSpec((1, SUBLANE, W), lambda b, g, i: (b, jnp.maximum(i * rb - 1, 0), off * ng + g))

    def convw(off):
        return pl.BlockSpec((CONV_WIDTH, W), lambda b, g, i: (0, off * ng + g))

    return pl.pallas_call(
        _gdn_body,
        name="gdn_delta_rule",
        grid=(B, ng, S // R),
        in_specs=[cur(0), cur(1), cur(2), cur(3), prev(0), prev(1), prev(2),
                  convw(0), convw(1), convw(2),
                  pl.BlockSpec((1, R, LANE), lambda b, g, i: (b, i, g)),
                  pl.BlockSpec((1, HEAD_DIM), lambda b, g, i: (0, 0)),
                  pl.BlockSpec((R, R), lambda b, g, i: (0, 0)),
                  pl.BlockSpec((4, R, R), lambda b, g, i: (0, 0, 0))],
        out_specs=pl.BlockSpec((1, R, W), lambda b, g, i: (b, i, g)),
        out_shape=jax.ShapeDtypeStruct((B, S, MIX_WIDTH), BF16),
        scratch_shapes=[pltpu.VMEM((G, HEAD_DIM, HEAD_DIM), F32)],
        compiler_params=_params("parallel", "parallel", "arbitrary"),
    )(h3, h3, h3, h3, h3, h3, h3, conv_w, conv_w, conv_w, gates, o_norm, *_gdn_block_masks(R))


def _rotate_half_cols(w):
    half = w.shape[-1] // 2
    return jnp.concatenate([-w[..., half:], w[..., :half]], -1)


def _mla_weights(w_in, w_uq, w_ukv):
    k_r = w_in[:, Q_LORA + KV_LORA:Q_LORA + KV_LORA + QK_ROPE]
    q_mem = w_in[:, Q_LORA + KV_LORA + QK_ROPE:]
    w_tail = jnp.concatenate([q_mem, k_r, _rotate_half_cols(k_r)], 1).astype(BF16)

    uq = w_uq.reshape(Q_LORA, N_MIX_HEADS, QK_NOPE + QK_ROPE)
    nope, rope = uq[..., :QK_NOPE], uq[..., QK_NOPE:]
    zeros = jnp.zeros_like(rope)
    main = jnp.concatenate([nope, rope, zeros], -1).reshape(Q_LORA, -1)
    rot = jnp.concatenate([_rotate_half_cols(rope), zeros], -1).reshape(Q_LORA, -1)
    w_uq_p = jnp.concatenate([main, rot], 1).astype(BF16)

    ukv = w_ukv.reshape(KV_LORA, N_MIX_HEADS, QK_NOPE + HEAD_DIM)
    w_ukv_p = jnp.concatenate([ukv[..., :QK_NOPE].reshape(KV_LORA, -1),
                               ukv[..., QK_NOPE:].reshape(KV_LORA, -1)], 1).astype(BF16)
    return w_tail, w_uq_p, w_ukv_p


def _group_lanes(a, b):
    G = GDN_HEADS
    lead = a.shape[:-1]
    a = a.reshape(lead + (N_MIX_HEADS // G, G))
    b = b.reshape(lead + (N_MIX_HEADS // G, G))
    pad = jnp.zeros(lead + (N_MIX_HEADS // G, LANE - 2 * G), a.dtype)
    return jnp.concatenate([a, b, pad], -1).reshape(lead + (-1,))


def _gdn_in_weights(w_in, a_log, dt_bias):
    D = w_in.shape[0]
    a = w_in[:, 4 * MIX_WIDTH:4 * MIX_WIDTH + N_MIX_HEADS]
    b = w_in[:, 4 * MIX_WIDTH + N_MIX_HEADS:4 * MIX_WIDTH + 2 * N_MIX_HEADS]
    q_mem = w_in[:, 4 * MIX_WIDTH + 2 * N_MIX_HEADS:]
    gate_w = _group_lanes(a, b)
    pad = jnp.zeros((D, MEM_WIDTH - gate_w.shape[1]), w_in.dtype)
    w_tail = jnp.concatenate([q_mem, gate_w, pad], 1).astype(BF16)

    def lanes(va, vb):
        v = _group_lanes(va, vb)
        return jnp.pad(v, (0, MEM_WIDTH - v.shape[0]))[None, :]

    zero, one = jnp.zeros_like(a_log), jnp.ones_like(a_log)
    return w_tail, (lanes(a_log, zero), lanes(dt_bias, zero), lanes(one, 2.0 * one))


def _pad_lanes(v):
    return jnp.pad(v, (0, LANE - v.shape[0]))[None, :]


def kernel(x, mem, positions, mla_w_in, mla_q_norm, mla_w_uq, mla_kv_norm, mla_w_ukv, gdn_w_in, gdn_conv, gdn_a_log, gdn_dt_bias, gdn_o_norm, mem_w_kv, w_out, ln1_g, ln1_b, mlp_w1, mlp_w2, ln2_g, ln2_b):
    B, S, D = x.shape
    T = B * S
    depth = w_out.shape[0]
    alpha = (2 * depth) ** 0.25

    inv_freq = 1.0 / (ROPE_THETA ** (jnp.arange(0, QK_ROPE, 2, dtype=F32) / QK_ROPE))
    freq = _pad_lanes(jnp.concatenate([inv_freq, inv_freq]))
    cosp, sinp = _rope_tables(positions.reshape(T, 1), freq)

    xf = x.reshape(T, D)
    xb = xf
    mem2 = mem.reshape(B * mem.shape[1], D)
    mem_w_kv_b, w_out_b = mem_w_kv.astype(BF16), w_out.astype(BF16)
    mlp_w1_b, mlp_w2_b = mlp_w1.astype(BF16), mlp_w2.astype(BF16)
    mla_w_in_b, gdn_w_in_b = mla_w_in.astype(BF16), gdn_w_in.astype(BF16)
    for i in range(depth):
        j = i // 2
        mem_kv = _matmul(mem2, mem_w_kv_b, F32, tm=256, tn=512, layer=i, n_cols=2 * MEM_WIDTH)
        mem_kv = mem_kv.reshape(B, mem.shape[1], 2 * MEM_WIDTH)
        if i % 2 == 0:
            w_tail, w_uq_p, w_ukv_p = _mla_weights(mla_w_in[j], mla_w_uq[j], mla_w_ukv[j])
            h = _matmul(xb, mla_w_in_b, F32, tm=1024, tn=Q_LORA + KV_LORA, layer=j, n_cols=Q_LORA + KV_LORA)
            h_tail = _matmul(xb, w_tail, F32, tm=1024, tn=w_tail.shape[1])
            q = _q_up(h, mla_q_norm[j][None, :], w_uq_p, cosp, sinp, (QK_NOPE + QK_ROPE) ** -0.5 * LOG2_E)
            kn, v, krp = _kv_up(h, h_tail, mla_kv_norm[j][None, :], w_ukv_p, cosp, sinp)
            mix = _flash(q.reshape(B, S, -1), kn.reshape(B, S, -1), krp.reshape(B, S, -1),
                         v.reshape(B, S, -1)).reshape(T, MIX_WIDTH)
        else:
            w_tail, lane_params = _gdn_in_weights(gdn_w_in[j], gdn_a_log[j], gdn_dt_bias[j])
            h = _matmul(xb, gdn_w_in_b, F32, tm=1024, tn=MIX_WIDTH, layer=j, n_cols=4 * MIX_WIDTH)
            h_tail = _matmul(xb, w_tail, F32, tm=1024, tn=w_tail.shape[1])
            gates = _gdn_gates(h_tail, lane_params)
            mix = _gdn_mix(h.reshape(B, S, -1), gdn_conv[j], gates.reshape(B, S, -1),
                           gdn_o_norm[j][None, :]).reshape(T, MIX_WIDTH)
        mem_o = _mem_attention(h_tail.reshape(B, S, -1), mem_kv)
        xf, xb = _outproj_ln(mix, mem_o.reshape(T, MEM_WIDTH), w_out_b, i, xf,
                             ln1_g[i][None, :], ln1_b[i][None, :], alpha)
        xf, xb = _mlp_ln(xf, xb, mlp_w1_b, mlp_w2_b, i, ln2_g[i][None, :], ln2_b[i][None, :], alpha)
    return xf.reshape(B, S, D)
```

```python
import functools

import jax
import jax.numpy as jnp
from jax import lax
from jax.experimental import pallas as pl
from jax.experimental.pallas import tpu as pltpu

F32 = jnp.float32
BF16 = jnp.bfloat16

HEAD_DIM = 128
N_MIX_HEADS = 12
N_MEM_HEADS = 4
MIX_WIDTH = N_MIX_HEADS * HEAD_DIM
MEM_WIDTH = N_MEM_HEADS * HEAD_DIM
Q_LORA = 512
KV_LORA = 512
QK_NOPE = 128
QK_ROPE = 64
ROPE_THETA = 10000.0
CONV_WIDTH = 4
CHUNK = 64
LN_EPS = 1e-5
RMS_EPS = 1e-6
L2_EPS = 1e-6

LANE = 128
SUBLANE = 8
VMEM_LIMIT = 56 * 1024 * 1024

GDN_ROWS = 128
GDN_HEADS = 12
LOG2_E = 1.4426950408889634
MASKED_LOG = -1e30


def _params(*sem):
    return pltpu.CompilerParams(dimension_semantics=sem, vmem_limit_bytes=VMEM_LIMIT)


def _dot(a, b):
    return jnp.dot(a, b, preferred_element_type=F32)


def _dot_nt(a, b):
    return lax.dot_general(a, b, (((1,), (1,)), ((), ())), preferred_element_type=F32)


def _dot_tn(a, b):
    return lax.dot_general(a, b, (((0,), (0,)), ((), ())), preferred_element_type=F32)


def _layer_norm(r, g, b):
    mu = jnp.mean(r, -1, keepdims=True)
    d = r - mu
    var = jnp.mean(d * d, -1, keepdims=True)
    return d * lax.rsqrt(var + LN_EPS) * g + b


def _rms_norm(x, g):
    return x * lax.rsqrt(jnp.mean(x * x, -1, keepdims=True) + RMS_EPS) * g


def _silu(x):
    return x / (1.0 + jnp.exp(-x))


def _mm_body(x_ref, w_ref, o_ref):
    o_ref[...] = _dot(x_ref[...].astype(BF16), w_ref[...]).astype(o_ref.dtype)


def _matmul(x, w, out_dtype, tm, tn, layer=None, n_cols=None):
    M, K = x.shape
    if layer is None:
        N = w.shape[1]
        w_spec = pl.BlockSpec((K, tn), lambda i, j: (0, j))
    else:
        N = n_cols
        w_spec = pl.BlockSpec((None, K, tn), lambda i, j: (layer, 0, j))
    return pl.pallas_call(
        _mm_body,
        name="matmul",
        grid=(M // tm, N // tn),
        in_specs=[pl.BlockSpec((tm, K), lambda i, j: (i, 0)), w_spec],
        out_specs=pl.BlockSpec((tm, tn), lambda i, j: (i, j)),
        out_shape=jax.ShapeDtypeStruct((M, N), out_dtype),
        compiler_params=_params("parallel", "parallel"),
    )(x, w)


def _outproj_body(mix_ref, mem_ref, wa_ref, wb_ref, x_ref, g_ref, b_ref, o_ref, ob_ref, *, alpha):
    y = _dot(mix_ref[...], wa_ref[...]) + _dot(mem_ref[...], wb_ref[...])
    o = _layer_norm(alpha * x_ref[...] + y, g_ref[...], b_ref[...])
    o_ref[...] = o
    ob_ref[...] = o.astype(BF16)


def _outproj_ln(mix, memo, w_all, layer, x, g, b, alpha, tm=512):
    M, D = x.shape
    mem_row_blk = mix.shape[1] // memo.shape[1]
    return pl.pallas_call(
        functools.partial(_outproj_body, alpha=alpha),
        name="outproj_ln",
        grid=(M // tm,),
        in_specs=[pl.BlockSpec((tm, mix.shape[1]), lambda i: (i, 0)),
                  pl.BlockSpec((tm, memo.shape[1]), lambda i: (i, 0)),
                  pl.BlockSpec((None, mix.shape[1], D), lambda i: (layer, 0, 0)),
                  pl.BlockSpec((None, memo.shape[1], D), lambda i: (layer, mem_row_blk, 0)),
                  pl.BlockSpec((tm, D), lambda i: (i, 0)),
                  pl.BlockSpec((1, D), lambda i: (0, 0)),
                  pl.BlockSpec((1, D), lambda i: (0, 0))],
        out_specs=[pl.BlockSpec((tm, D), lambda i: (i, 0))] * 2,
        out_shape=[jax.ShapeDtypeStruct((M, D), F32), jax.ShapeDtypeStruct((M, D), BF16)],
        compiler_params=_params("parallel"),
    )(mix, memo, w_all, w_all, x, g, b)


def _mlp_body(x_ref, xb_ref, w1_ref, w2_ref, g_ref, b_ref, o_ref, ob_ref, acc_ref, *, alpha):
    f = pl.program_id(1)

    @pl.when(f == 0)
    def _():
        acc_ref[...] = jnp.zeros_like(acc_ref)

    h = jnp.maximum(_dot(xb_ref[...], w1_ref[...]), 0.0)
    acc_ref[...] += _dot((h * h).astype(BF16), w2_ref[...])

    @pl.when(f == pl.num_programs(1) - 1)
    def _():
        o = _layer_norm(alpha * x_ref[...] + acc_ref[...], g_ref[...], b_ref[...])
        o_ref[...] = o
        ob_ref[...] = o.astype(BF16)


def _mlp_ln(x, xb, w1, w2, layer, g, b, alpha, tm=512, tf=1024):
    M, D = x.shape
    FF = w1.shape[2]
    return pl.pallas_call(
        functools.partial(_mlp_body, alpha=alpha),
        name="mlp_ln",
        grid=(M // tm, FF // tf),
        in_specs=[pl.BlockSpec((tm, D), lambda i, f: (i, 0)),
                  pl.BlockSpec((tm, D), lambda i, f: (i, 0)),
                  pl.BlockSpec((None, D, tf), lambda i, f: (layer, 0, f)),
                  pl.BlockSpec((None, tf, D), lambda i, f: (layer, f, 0)),
                  pl.BlockSpec((1, D), lambda i, f: (0, 0)),
                  pl.BlockSpec((1, D), lambda i, f: (0, 0))],
        out_specs=[pl.BlockSpec((tm, D), lambda i, f: (i, 0))] * 2,
        out_shape=[jax.ShapeDtypeStruct((M, D), F32), jax.ShapeDtypeStruct((M, D), BF16)],
        scratch_shapes=[pltpu.VMEM((tm, D), F32)],
        compiler_params=_params("parallel", "arbitrary"),
    )(x, xb, w1, w2, g, b)


def _rope_body(pos_ref, freq_ref, cos_ref, sin_ref):
    ang = pos_ref[...].astype(F32) * freq_ref[...]
    lane = lax.broadcasted_iota(jnp.int32, ang.shape, 1)
    keep = lane < QK_ROPE
    cos_ref[...] = jnp.where(keep, jnp.cos(ang), 0.0)
    sin_ref[...] = jnp.where(keep, jnp.sin(ang), 0.0)


def _rope_tables(pos, freq, tm=1024):
    T = pos.shape[0]
    shp = jax.ShapeDtypeStruct((T, LANE), F32)
    return pl.pallas_call(
        _rope_body,
        name="rope_tables",
        grid=(T // tm,),
        in_specs=[pl.BlockSpec((tm, 1), lambda i: (i, 0)),
                  pl.BlockSpec((1, LANE), lambda i: (0, 0))],
        out_specs=[pl.BlockSpec((tm, LANE), lambda i: (i, 0))] * 2,
        out_shape=[shp, shp],
        compiler_params=_params("parallel"),
    )(pos, freq)


def _qup_body(c_ref, g_ref, w_ref, cos_ref, sin_ref, o_ref, *, scale):
    cn = _rms_norm(c_ref[...], g_ref[...]).astype(BF16)
    r = _dot(cn, w_ref[...])
    cos = cos_ref[...] * scale
    sin = sin_ref[...] * scale
    hw = 2 * LANE
    rot0 = N_MIX_HEADS * hw
    for h in range(N_MIX_HEADS):
        o_ref[:, h * hw:h * hw + LANE] = (r[:, h * hw:h * hw + LANE] * scale).astype(o_ref.dtype)
        rope = r[:, h * hw + LANE:(h + 1) * hw] * cos + r[:, rot0 + h * LANE:rot0 + (h + 1) * LANE] * sin
        o_ref[:, h * hw + LANE:(h + 1) * hw] = rope.astype(o_ref.dtype)


def _q_up(h, g, w, cosp, sinp, scale, tm=512):
    T = h.shape[0]
    N = N_MIX_HEADS * 2 * LANE
    return pl.pallas_call(
        functools.partial(_qup_body, scale=scale),
        name="mla_q_up",
        grid=(T // tm,),
        in_specs=[pl.BlockSpec((tm, Q_LORA), lambda i: (i, 0)),
                  pl.BlockSpec((1, Q_LORA), lambda i: (0, 0)),
                  pl.BlockSpec(w.shape, lambda i: (0, 0)),
                  pl.BlockSpec((tm, LANE), lambda i: (i, 0)),
                  pl.BlockSpec((tm, LANE), lambda i: (i, 0))],
        out_specs=pl.BlockSpec((tm, N), lambda i: (i, 0)),
        out_shape=jax.ShapeDtypeStruct((T, N), BF16),
        compiler_params=_params("parallel"),
    )(h, g, w, cosp, sinp)


def _kvup_body(c_ref, kr_ref, g_ref, w_ref, cos_ref, sin_ref, kn_ref, v_ref, krp_ref):
    cn = _rms_norm(c_ref[...], g_ref[...]).astype(BF16)
    r = _dot(cn, w_ref[...])
    kn_ref[...] = r[:, :MIX_WIDTH].astype(kn_ref.dtype)
    v_ref[...] = r[:, MIX_WIDTH:].astype(v_ref.dtype)
    kr = kr_ref[...]
    swapped = pltpu.roll(kr, QK_ROPE, 1)
    krp_ref[...] = (kr * cos_ref[...] + swapped * sin_ref[...]).astype(krp_ref.dtype)


def _kv_up(h, h_tail, g, w, cosp, sinp, tm=512):
    T = h.shape[0]
    kr_blk = MEM_WIDTH // LANE
    return pl.pallas_call(
        _kvup_body,
        name="mla_kv_up",
        grid=(T // tm,),
        in_specs=[pl.BlockSpec((tm, KV_LORA), lambda i: (i, 1)),
                  pl.BlockSpec((tm, LANE), lambda i: (i, kr_blk)),
                  pl.BlockSpec((1, KV_LORA), lambda i: (0, 0)),
                  pl.BlockSpec(w.shape, lambda i: (0, 0)),
                  pl.BlockSpec((tm, LANE), lambda i: (i, 0)),
                  pl.BlockSpec((tm, LANE), lambda i: (i, 0))],
        out_specs=[pl.BlockSpec((tm, MIX_WIDTH), lambda i: (i, 0)),
                   pl.BlockSpec((tm, MIX_WIDTH), lambda i: (i, 0)),
                   pl.BlockSpec((tm, LANE), lambda i: (i, 0))],
        out_shape=[jax.ShapeDtypeStruct((T, MIX_WIDTH), BF16),
                   jax.ShapeDtypeStruct((T, MIX_WIDTH), BF16),
                   jax.ShapeDtypeStruct((T, LANE), BF16)],
        compiler_params=_params("parallel"),
    )(h, h_tail, g, w, cosp, sinp)


def _flash_body(q_ref, kn_ref, kr_ref, v_ref, o_ref, m_ref, l_ref, acc_ref, *, tq, nh):
    qi = pl.program_id(2)
    qw = 2 * LANE

    hs = range(nh)
    hd = [slice(h * HEAD_DIM, (h + 1) * HEAD_DIM) for h in hs]

    def scores(j):
        start = pl.multiple_of(j * tq, tq)
        kr = kr_ref[0, pl.ds(start, tq), :]
        ks = [jnp.concatenate([kn_ref[0, pl.ds(start, tq), hd[h]], kr], axis=1) for h in hs]
        return [_dot_nt(q_ref[0, :, h * qw:(h + 1) * qw], ks[h]) for h in hs]

    def update(js, ss):
        nb = range(len(js))
        starts = [pl.multiple_of(j * tq, tq) for j in js]
        rep = lambda col: jnp.broadcast_to(col, (tq, LANE))
        tile = lambda x: jnp.concatenate([x] * (tq // LANE), axis=1)
        m = [m_ref[h] for h in hs]
        m_new = list(m)
        for b in nb:
            m_new = [jnp.maximum(m_new[h], rep(jnp.max(ss[b][h], -1, keepdims=True))) for h in hs]
        a = [jnp.exp2(m[h] - m_new[h]) for h in hs]
        p = [[jnp.exp2(ss[b][h] - tile(m_new[h])) for h in hs] for b in nb]
        l = [a[h] * l_ref[h] for h in hs]
        acc = [a[h] * acc_ref[h] for h in hs]
        for b in nb:
            l = [l[h] + rep(jnp.sum(p[b][h], -1, keepdims=True)) for h in hs]
        pv = [[_dot(p[b][h].astype(BF16), v_ref[0, pl.ds(starts[b], tq), hd[h]]) for h in hs] for b in nb]
        for b in nb:
            acc = [acc[h] + pv[b][h] for h in hs]
        for h in hs:
            m_ref[h] = m_new[h]
            l_ref[h] = l[h]
            acc_ref[h] = acc[h]
        return l, acc

    m_ref[...] = jnp.full(m_ref.shape, -jnp.inf, F32)
    l_ref[...] = jnp.zeros(l_ref.shape, F32)
    acc_ref[...] = jnp.zeros(acc_ref.shape, F32)

    def pair(j):
        return [j, j + 1], [scores(j), scores(j + 1)]

    @pl.loop(0, qi // 4)
    def _(j4):
        first, second = pair(4 * j4), pair(4 * j4 + 2)
        update(*first)
        update(*second)

    @pl.when(qi % 4 >= 2)
    def _():
        update(*pair(4 * (qi // 4)))

    @pl.when(qi % 2 == 1)
    def _():
        update([qi - 1], [scores(qi - 1)])

    row = lax.broadcasted_iota(jnp.int32, (tq, tq), 0)
    col = lax.broadcasted_iota(jnp.int32, (tq, tq), 1)
    s = [jnp.where(col <= row, sh, -jnp.inf) for sh in scores(qi)]
    l, acc = update([qi], [s])
    o_ref[0] = jnp.concatenate([acc[h] / l[h] for h in hs], axis=1).astype(o_ref.dtype)


def _flash(q, kn, kr, v, tq=512, nh=2):
    B, S, _ = q.shape
    return pl.pallas_call(
        functools.partial(_flash_body, tq=tq, nh=nh),
        name="mla_flash",
        grid=(B, N_MIX_HEADS // nh, S // tq),
        in_specs=[pl.BlockSpec((1, tq, nh * 2 * LANE), lambda b, h, i: (b, i, h)),
                  pl.BlockSpec((1, S, nh * HEAD_DIM), lambda b, h, i: (b, 0, h)),
                  pl.BlockSpec((1, S, LANE), lambda b, h, i: (b, 0, 0)),
                  pl.BlockSpec((1, S, nh * HEAD_DIM), lambda b, h, i: (b, 0, h))],
        out_specs=pl.BlockSpec((1, tq, nh * HEAD_DIM), lambda b, h, i: (b, i, h)),
        out_shape=jax.ShapeDtypeStruct((B, S, MIX_WIDTH), BF16),
        scratch_shapes=[pltpu.VMEM((nh, tq, LANE), F32), pltpu.VMEM((nh, tq, LANE), F32),
                        pltpu.VMEM((nh, tq, HEAD_DIM), F32)],
        compiler_params=_params("parallel", "parallel", "arbitrary"),
    )(q, kn, kr, v)


def _memattn_body(q_ref, kv_ref, o_ref, *, scale):
    for h in range(N_MEM_HEADS):
        q = (q_ref[0, :, h * HEAD_DIM:(h + 1) * HEAD_DIM] * scale).astype(BF16)
        k = kv_ref[0, :, h * HEAD_DIM:(h + 1) * HEAD_DIM].astype(BF16)
        v = kv_ref[0, :, MEM_WIDTH + h * HEAD_DIM:MEM_WIDTH + (h + 1) * HEAD_DIM].astype(BF16)
        s = _dot_nt(q, k)
        p = jnp.exp(s - jnp.max(s, -1, keepdims=True))
        o = _dot(p.astype(BF16), v) / jnp.sum(p, -1, keepdims=True)
        o_ref[0, :, h * HEAD_DIM:(h + 1) * HEAD_DIM] = o.astype(o_ref.dtype)


def _mem_attention(h3, mem_kv, tm=512):
    B, S, _ = h3.shape
    M = mem_kv.shape[1]
    return pl.pallas_call(
        functools.partial(_memattn_body, scale=HEAD_DIM ** -0.5),
        name="mem_attention",
        grid=(B, S // tm),
        in_specs=[pl.BlockSpec((1, tm, MEM_WIDTH), lambda b, i: (b, i, 0)),
                  pl.BlockSpec((1, M, 2 * MEM_WIDTH), lambda b, i: (b, 0, 0))],
        out_specs=pl.BlockSpec((1, tm, MEM_WIDTH), lambda b, i: (b, i, 0)),
        out_shape=jax.ShapeDtypeStruct((B, S, MEM_WIDTH), BF16),
        compiler_params=_params("parallel", "parallel"),
    )(h3, mem_kv)


def _gate_body(ab_ref, alog_ref, dt_ref, kind_ref, o_ref):
    ab = ab_ref[...]
    kind = kind_ref[...]
    g = -jnp.exp(alog_ref[...]) * jax.nn.softplus(ab + dt_ref[...])
    o_ref[...] = jnp.where(kind == 1.0, g, jnp.where(kind == 2.0, jax.nn.sigmoid(ab), 0.0))


def _gdn_gates(h_tail, lane_params, tm=1024):
    T = h_tail.shape[0]
    W = MEM_WIDTH
    row = pl.BlockSpec((1, W), lambda i: (0, 0))
    return pl.pallas_call(
        _gate_body,
        name="gdn_gates",
        grid=(T // tm,),
        in_specs=[pl.BlockSpec((tm, W), lambda i: (i, 1)), row, row, row],
        out_specs=pl.BlockSpec((tm, W), lambda i: (i, 0)),
        out_shape=jax.ShapeDtypeStruct((T, W), F32),
        compiler_params=_params("parallel"),
    )(h_tail, *lane_params)


def _unit_lower_inverses(Ls, masks):
    eye, neg_strict16, blk32_only, blk64_only = masks
    idx = range(len(Ls))

    def bf(xs):
        return [x.astype(BF16) for x in xs]

    n1 = [L * neg_strict16 for L in Ls]
    nb = bf(n1)
    p = [eye + x for x in n1]
    for _ in range(3):
        nb = bf([_dot(nb[i], nb[i]) for i in idx])
        pb = bf(p)
        p = [p[i] + _dot(pb[i], nb[i]) for i in idx]
    for level in (blk32_only, blk64_only):
        pb = bf(p)
        cb = bf([L * level for L in Ls])
        t = bf([_dot(pb[i], cb[i]) for i in idx])
        p = [p[i] - _dot(t[i], pb[i]) for i in idx]
    return p


def _causal_conv_silu(x_ref, ln, prev, w):
    assert CONV_WIDTH == 4
    t0, t1, t2, t3 = (w[CONV_WIDTH - 1 - d:CONV_WIDTH - d, :] for d in range(CONV_WIDTH))
    xe = jnp.concatenate([prev, x_ref[0, :, ln]], axis=0)
    xs = pltpu.roll(xe, 1, 0)
    y = (xe * t0 + xs * t1) + pltpu.roll(xe * t2 + xs * t3, 2, 0)
    return _silu(y[SUBLANE:])


def _l2_normalize(x):
    return x * lax.rsqrt(jnp.sum(x * x, -1, keepdims=True) + L2_EPS)


def _gdn_body(q_ref, k_ref, v_ref, z_ref, qp_ref, kp_ref, vp_ref, wq_ref, wk_ref, wv_ref,
              gate_ref, onorm_ref, mtril_ref, masks_ref, o_ref, state_ref):
    R = q_ref.shape[1]
    nc = R // CHUNK
    step = pl.program_id(2)

    @pl.when(step == 0)
    def _():
        state_ref[...] = jnp.zeros_like(state_ref)

    not_first = (step > 0).astype(F32)
    row = lax.broadcasted_iota(jnp.int32, (R, R), 0)
    col = lax.broadcasted_iota(jnp.int32, (R, R), 1)

    tril = jnp.logical_and((row >> 6) == (col >> 6), col <= row)
    m_tril = mtril_ref[...]
    masks = tuple(masks_ref[i] for i in range(4))

    def split3(a):
        a1 = a.astype(BF16)
        r1 = a - a1.astype(F32)
        a2 = r1.astype(BF16)
        a3 = (r1 - a2.astype(F32)).astype(BF16)
        return a1, a2, a3

    def lane_bcast(a, lane):
        return jnp.broadcast_to(a[:, lane:lane + 1], (R, HEAD_DIM))

    gates = gate_ref[0]
    gparts = split3(gates)
    gc_all = _dot(m_tril, gparts[0]) + _dot(m_tril, gparts[1]) + _dot(m_tril, gparts[2])
    gc_rows = gc_all.T
    g_end_all = jnp.concatenate(
        [jnp.broadcast_to(gc_all[(c + 1) * CHUNK - 1:(c + 1) * CHUNK, :], (CHUNK, LANE)) for c in range(nc)], axis=0)
    egc_all = jnp.exp(gc_all)
    kdec_all = jnp.exp(g_end_all - gc_all)
    glast_all = jnp.exp(g_end_all)

    heads = range(GDN_HEADS)
    lns = [slice(hh * HEAD_DIM, (hh + 1) * HEAD_DIM) for hh in heads]
    q, k, v, beta, egc, decay, kb, kbf, lmat = [], [], [], [], [], [], [], [], []
    for hh, ln in zip(heads, lns):
        xq = _causal_conv_silu(q_ref, ln, qp_ref[0, :, ln] * not_first, wq_ref[:, ln])
        xk = _causal_conv_silu(k_ref, ln, kp_ref[0, :, ln] * not_first, wk_ref[:, ln])
        v.append(_causal_conv_silu(v_ref, ln, vp_ref[0, :, ln] * not_first, wv_ref[:, ln]))
        q.append(_l2_normalize(xq) * (HEAD_DIM ** -0.5))
        k.append(_l2_normalize(xk))
        gc = lane_bcast(gc_all, hh)
        beta.append(lane_bcast(gates, GDN_HEADS + hh))
        egc.append(lane_bcast(egc_all, hh))
        diff = jnp.concatenate([gc] * (R // HEAD_DIM), axis=1) - gc_rows[hh:hh + 1, :]
        decay.append(jnp.exp(jnp.where(tril, diff, MASKED_LOG)))
        kb.append(k[hh] * beta[hh])
        kbf.append(k[hh].astype(BF16))
        lmat.append(_dot_nt(kb[hh].astype(BF16), kbf[hh]) * decay[hh])

    tinv = _unit_lower_inverses(lmat, masks)

    u, w, a_qk, q_dec, k_dec = [], [], [], [], []
    for hh in heads:
        rhs = jnp.concatenate([v[hh] * beta[hh], kb[hh] * egc[hh]], axis=1).astype(BF16)
        sol = _dot(tinv[hh].astype(BF16), rhs)
        u.append(sol[:, :HEAD_DIM])
        w.append(sol[:, HEAD_DIM:].astype(BF16))
        a_qk.append((_dot_nt(q[hh].astype(BF16), kbf[hh]) * decay[hh]).astype(BF16))
        q_dec.append((q[hh] * egc[hh]).astype(BF16))
        k_dec.append((k[hh] * lane_bcast(kdec_all, hh)).astype(BF16))

    state = [state_ref[hh] for hh in heads]
    v_new = [[] for _ in heads]
    o_inter = [[] for _ in heads]
    for c in range(nc):
        rows = slice(c * CHUNK, (c + 1) * CHUNK)
        for hh in heads:
            both = _dot(jnp.concatenate([w[hh][rows], q_dec[hh][rows]], axis=0), state[hh].astype(BF16))
            vn = u[hh][rows] - both[:CHUNK]
            o_inter[hh].append(both[CHUNK:])
            g_last = glast_all[c * CHUNK:c * CHUNK + 1, hh:hh + 1]
            state[hh] = state[hh] * g_last + _dot_tn(k_dec[hh][rows], vn.astype(BF16))
            v_new[hh].append(vn)
    state_ref[...] = jnp.stack(state)

    outs = []
    for hh, ln in zip(heads, lns):
        vn_all = jnp.concatenate(v_new[hh], axis=0).astype(BF16)
        o = jnp.concatenate(o_inter[hh], axis=0) + _dot(a_qk[hh], vn_all)
        outs.append(_rms_norm(o, onorm_ref[...]) * _silu(z_ref[0, :, ln]))
    o_ref[0] = jnp.concatenate(outs, axis=1).astype(o_ref.dtype)


def _gdn_block_masks(R):
    row = lax.broadcasted_iota(jnp.int32, (R, R), 0)
    col = lax.broadcasted_iota(jnp.int32, (R, R), 1)

    def same(n):
        return (row // n) == (col // n)

    one = lambda c: jnp.where(c, 1.0, 0.0).astype(F32)
    tril = one(jnp.logical_and(same(CHUNK), col <= row)).astype(BF16)
    eye = one(row == col)
    neg_strict16 = -one(jnp.logical_and(same(16), col < row))
    blk32_only = one(jnp.logical_and(same(32), jnp.logical_not(same(16))))
    blk64_only = one(jnp.logical_and(same(CHUNK), jnp.logical_not(same(32))))
    return tril, jnp.stack([eye, neg_strict16, blk32_only, blk64_only])


def _gdn_mix(h3, conv_w, gates, o_norm):
    B, S, _ = h3.shape
    R, G = GDN_ROWS, GDN_HEADS
    W = G * HEAD_DIM
    ng = N_MIX_HEADS // G
    rb = R // SUBLANE

    def cur(off):
        return pl.BlockSpec((1, R, W), lambda b, g, i: (b, i, off * ng + g))

    def prev(off):
        return pl.BlockSpec((1, SUBLANE, W), lambda b, g, i: (b, jnp.maximum(i * rb - 1, 0), off * ng + g))

    def convw(off):
        return pl.BlockSpec((CONV_WIDTH, W), lambda b, g, i: (0, off * ng + g))

    return pl.pallas_call(
        _gdn_body,
        name="gdn_delta_rule",
        grid=(B, ng, S // R),
        in_specs=[cur(0), cur(1), cur(2), cur(3), prev(0), prev(1), prev(2),
                  convw(0), convw(1), convw(2),
                  pl.BlockSpec((1, R, LANE), lambda b, g, i: (b, i, g)),
                  pl.BlockSpec((1, HEAD_DIM), lambda b, g, i: (0, 0)),
                  pl.BlockSpec((R, R), lambda b, g, i: (0, 0)),
                  pl.BlockSpec((4, R, R), lambda b, g, i: (0, 0, 0))],
        out_specs=pl.BlockSpec((1, R, W), lambda b, g, i: (b, i, g)),
        out_shape=jax.ShapeDtypeStruct((B, S, MIX_WIDTH), BF16),
        scratch_shapes=[pltpu.VMEM((G, HEAD_DIM, HEAD_DIM), F32)],
        compiler_params=_params("parallel", "parallel", "arbitrary"),
    )(h3, h3, h3, h3, h3, h3, h3, conv_w, conv_w, conv_w, gates, o_norm, *_gdn_block_masks(R))


def _rotate_half_cols(w):
    half = w.shape[-1] // 2
    return jnp.concatenate([-w[..., half:], w[..., :half]], -1)


def _mla_weights(w_in, w_uq, w_ukv):
    k_r = w_in[:, Q_LORA + KV_LORA:Q_LORA + KV_LORA + QK_ROPE]
    q_mem = w_in[:, Q_LORA + KV_LORA + QK_ROPE:]
    w_tail = jnp.concatenate([q_mem, k_r, _rotate_half_cols(k_r)], 1).astype(BF16)

    uq = w_uq.reshape(Q_LORA, N_MIX_HEADS, QK_NOPE + QK_ROPE)
    nope, rope = uq[..., :QK_NOPE], uq[..., QK_NOPE:]
    zeros = jnp.zeros_like(rope)
    main = jnp.concatenate([nope, rope, zeros], -1).reshape(Q_LORA, -1)
    rot = jnp.concatenate([_rotate_half_cols(rope), zeros], -1).reshape(Q_LORA, -1)
    w_uq_p = jnp.concatenate([main, rot], 1).astype(BF16)

    ukv = w_ukv.reshape(KV_LORA, N_MIX_HEADS, QK_NOPE + HEAD_DIM)
    w_ukv_p = jnp.concatenate([ukv[..., :QK_NOPE].reshape(KV_LORA, -1),
                               ukv[..., QK_NOPE:].reshape(KV_LORA, -1)], 1).astype(BF16)
    return w_tail, w_uq_p, w_ukv_p


def _group_lanes(a, b):
    G = GDN_HEADS
    lead = a.shape[:-1]
    a = a.reshape(lead + (N_MIX_HEADS // G, G))
    b = b.reshape(lead + (N_MIX_HEADS // G, G))
    pad = jnp.zeros(lead + (N_MIX_HEADS // G, LANE - 2 * G), a.dtype)
    return jnp.concatenate([a, b, pad], -1).reshape(lead + (-1,))


def _gdn_in_weights(w_in, a_log, dt_bias):
    D = w_in.shape[0]
    a = w_in[:, 4 * MIX_WIDTH:4 * MIX_WIDTH + N_MIX_HEADS]
    b = w_in[:, 4 * MIX_WIDTH + N_MIX_HEADS:4 * MIX_WIDTH + 2 * N_MIX_HEADS]
    q_mem = w_in[:, 4 * MIX_WIDTH + 2 * N_MIX_HEADS:]
    gate_w = _group_lanes(a, b)
    pad = jnp.zeros((D, MEM_WIDTH - gate_w.shape[1]), w_in.dtype)
    w_tail = jnp.concatenate([q_mem, gate_w, pad], 1).astype(BF16)

    def lanes(va, vb):
        v = _group_lanes(va, vb)
        return jnp.pad(v, (0, MEM_WIDTH - v.shape[0]))[None, :]

    zero, one = jnp.zeros_like(a_log), jnp.ones_like(a_log)
    return w_tail, (lanes(a_log, zero), lanes(dt_bias, zero), lanes(one, 2.0 * one))


def _pad_lanes(v):
    return jnp.pad(v, (0, LANE - v.shape[0]))[None, :]


def kernel(x, mem, positions, mla_w_in, mla_q_norm, mla_w_uq, mla_kv_norm, mla_w_ukv, gdn_w_in, gdn_conv, gdn_a_log, gdn_dt_bias, gdn_o_norm, mem_w_kv, w_out, ln1_g, ln1_b, mlp_w1, mlp_w2, ln2_g, ln2_b):
    B, S, D = x.shape
    T = B * S
    depth = w_out.shape[0]
    alpha = (2 * depth) ** 0.25

    inv_freq = 1.0 / (ROPE_THETA ** (jnp.arange(0, QK_ROPE, 2, dtype=F32) / QK_ROPE))
    freq = _pad_lanes(jnp.concatenate([inv_freq, inv_freq]))
    cosp, sinp = _rope_tables(positions.reshape(T, 1), freq)

    xf = x.reshape(T, D)
    xb = xf
    mem2 = mem.reshape(B * mem.shape[1], D)
    mem_w_kv_b, w_out_b = mem_w_kv.astype(BF16), w_out.astype(BF16)
    mlp_w1_b, mlp_w2_b = mlp_w1.astype(BF16), mlp_w2.astype(BF16)
    mla_w_in_b = mla_w_in[:, :, :Q_LORA + KV_LORA].astype(BF16)
    gdn_w_in_b = gdn_w_in[:, :, :4 * MIX_WIDTH].astype(BF16)
    for i in range(depth):
        j = i // 2
        mem_kv = _matmul(mem2, mem_w_kv_b, F32, tm=256, tn=512, layer=i, n_cols=2 * MEM_WIDTH)
        mem_kv = mem_kv.reshape(B, mem.shape[1], 2 * MEM_WIDTH)
        if i % 2 == 0:
            w_tail, w_uq_p, w_ukv_p = _mla_weights(mla_w_in[j], mla_w_uq[j], mla_w_ukv[j])
            h = _matmul(xb, mla_w_in_b, F32, tm=1024, tn=Q_LORA + KV_LORA, layer=j, n_cols=Q_LORA + KV_LORA)
            h_tail = _matmul(xb, w_tail, F32, tm=1024, tn=w_tail.shape[1])
            q = _q_up(h, mla_q_norm[j][None, :], w_uq_p, cosp, sinp, (QK_NOPE + QK_ROPE) ** -0.5 * LOG2_E)
            kn, v, krp = _kv_up(h, h_tail, mla_kv_norm[j][None, :], w_ukv_p, cosp, sinp)
            mix = _flash(q.reshape(B, S, -1), kn.reshape(B, S, -1), krp.reshape(B, S, -1),
                         v.reshape(B, S, -1)).reshape(T, MIX_WIDTH)
        else:
            w_tail, lane_params = _gdn_in_weights(gdn_w_in[j], gdn_a_log[j], gdn_dt_bias[j])
            h = _matmul(xb, gdn_w_in_b, F32, tm=1024, tn=MIX_WIDTH, layer=j, n_cols=4 * MIX_WIDTH)
            h_tail = _matmul(xb, w_tail, F32, tm=1024, tn=w_tail.shape[1])
            gates = _gdn_gates(h_tail, lane_params)
            mix = _gdn_mix(h.reshape(B, S, -1), gdn_conv[j], gates.reshape(B, S, -1),
                           gdn_o_norm[j][None, :]).reshape(T, MIX_WIDTH)
        mem_o = _mem_attention(h_tail.reshape(B, S, -1), mem_kv)
        xf, xb = _outproj_ln(mix, mem_o.reshape(T, MEM_WIDTH), w_out_b, i, xf,
                             ln1_g[i][None, :], ln1_b[i][None, :], alpha)
        xf, xb = _mlp_ln(xf, xb, mlp_w1_b, mlp_w2_b, i, ln2_g[i][None, :], ln2_b[i][None, :], alpha)
    return xf.reshape(B, S, D)
```

```python
import functools

import jax
import jax.numpy as jnp
from jax import lax
from jax.experimental import pallas as pl
from jax.experimental.pallas import tpu as pltpu

F32 = jnp.float32
BF16 = jnp.bfloat16

HEAD_DIM = 128
N_MIX_HEADS = 12
N_MEM_HEADS = 4
MIX_WIDTH = N_MIX_HEADS * HEAD_DIM
MEM_WIDTH = N_MEM_HEADS * HEAD_DIM
Q_LORA = 512
KV_LORA = 512
QK_NOPE = 128
QK_ROPE = 64
ROPE_THETA = 10000.0
CONV_WIDTH = 4
CHUNK = 64
LN_EPS = 1e-5
RMS_EPS = 1e-6
L2_EPS = 1e-6

LANE = 128
SUBLANE = 8
VMEM_LIMIT = 56 * 1024 * 1024

GDN_ROWS = 128
GDN_HEADS = 12
LOG2_E = 1.4426950408889634
MASKED_LOG = -1e30


def _params(*sem):
    return pltpu.CompilerParams(dimension_semantics=sem, vmem_limit_bytes=VMEM_LIMIT)


def _dot(a, b):
    return jnp.dot(a, b, preferred_element_type=F32)


def _dot_nt(a, b):
    return lax.dot_general(a, b, (((1,), (1,)), ((), ())), preferred_element_type=F32)


def _dot_tn(a, b):
    return lax.dot_general(a, b, (((0,), (0,)), ((), ())), preferred_element_type=F32)


def _layer_norm(r, g, b):
    mu = jnp.mean(r, -1, keepdims=True)
    d = r - mu
    var = jnp.mean(d * d, -1, keepdims=True)
    return d * lax.rsqrt(var + LN_EPS) * g + b


def _rms_norm(x, g):
    return x * lax.rsqrt(jnp.mean(x * x, -1, keepdims=True) + RMS_EPS) * g


def _silu(x):
    return x / (1.0 + jnp.exp(-x))


def _mm_body(x_ref, w_ref, o_ref):
    o_ref[...] = _dot(x_ref[...].astype(BF16), w_ref[...]).astype(o_ref.dtype)


def _matmul(x, w, out_dtype, tm, tn, layer=None, n_cols=None):
    M, K = x.shape
    if layer is None:
        N = w.shape[1]
        w_spec = pl.BlockSpec((K, tn), lambda i, j: (0, j))
    else:
        N = n_cols
        w_spec = pl.BlockSpec((None, K, tn), lambda i, j: (layer, 0, j))
    return pl.pallas_call(
        _mm_body,
        name="matmul",
        grid=(M // tm, N // tn),
        in_specs=[pl.BlockSpec((tm, K), lambda i, j: (i, 0)), w_spec],
        out_specs=pl.BlockSpec((tm, tn), lambda i, j: (i, j)),
        out_shape=jax.ShapeDtypeStruct((M, N), out_dtype),
        compiler_params=_params("parallel", "parallel"),
    )(x, w)


def _outproj_body(mix_ref, mem_ref, wa_ref, wb_ref, x_ref, g_ref, b_ref, o_ref, ob_ref, *, alpha):
    y = _dot(mix_ref[...], wa_ref[...]) + _dot(mem_ref[...], wb_ref[...])
    o = _layer_norm(alpha * x_ref[...] + y, g_ref[...], b_ref[...])
    o_ref[...] = o
    ob_ref[...] = o.astype(BF16)


def _outproj_ln(mix, memo, w_all, layer, x, g, b, alpha, tm=512):
    M, D = x.shape
    mem_row_blk = mix.shape[1] // memo.shape[1]
    return pl.pallas_call(
        functools.partial(_outproj_body, alpha=alpha),
        name="outproj_ln",
        grid=(M // tm,),
        in_specs=[pl.BlockSpec((tm, mix.shape[1]), lambda i: (i, 0)),
                  pl.BlockSpec((tm, memo.shape[1]), lambda i: (i, 0)),
                  pl.BlockSpec((None, mix.shape[1], D), lambda i: (layer, 0, 0)),
                  pl.BlockSpec((None, memo.shape[1], D), lambda i: (layer, mem_row_blk, 0)),
                  pl.BlockSpec((tm, D), lambda i: (i, 0)),
                  pl.BlockSpec((1, D), lambda i: (0, 0)),
                  pl.BlockSpec((1, D), lambda i: (0, 0))],
        out_specs=[pl.BlockSpec((tm, D), lambda i: (i, 0))] * 2,
        out_shape=[jax.ShapeDtypeStruct((M, D), F32), jax.ShapeDtypeStruct((M, D), BF16)],
        compiler_params=_params("parallel"),
    )(mix, memo, w_all, w_all, x, g, b)


def _mlp_body(x_ref, xb_ref, w1_ref, w2_ref, g_ref, b_ref, o_ref, ob_ref, acc_ref, *, alpha):
    f = pl.program_id(1)

    @pl.when(f == 0)
    def _():
        acc_ref[...] = jnp.zeros_like(acc_ref)

    h = jnp.maximum(_dot(xb_ref[...], w1_ref[...]), 0.0)
    acc_ref[...] += _dot((h * h).astype(BF16), w2_ref[...])

    @pl.when(f == pl.num_programs(1) - 1)
    def _():
        o = _layer_norm(alpha * x_ref[...] + acc_ref[...], g_ref[...], b_ref[...])
        o_ref[...] = o
        ob_ref[...] = o.astype(BF16)


def _mlp_ln(x, xb, w1, w2, layer, g, b, alpha, tm=512, tf=1024):
    M, D = x.shape
    FF = w1.shape[2]
    return pl.pallas_call(
        functools.partial(_mlp_body, alpha=alpha),
        name="mlp_ln",
        grid=(M // tm, FF // tf),
        in_specs=[pl.BlockSpec((tm, D), lambda i, f: (i, 0)),
                  pl.BlockSpec((tm, D), lambda i, f: (i, 0)),
                  pl.BlockSpec((None, D, tf), lambda i, f: (layer, 0, f)),
                  pl.BlockSpec((None, tf, D), lambda i, f: (layer, f, 0)),
                  pl.BlockSpec((1, D), lambda i, f: (0, 0)),
                  pl.BlockSpec((1, D), lambda i, f: (0, 0))],
        out_specs=[pl.BlockSpec((tm, D), lambda i, f: (i, 0))] * 2,
        out_shape=[jax.ShapeDtypeStruct((M, D), F32), jax.ShapeDtypeStruct((M, D), BF16)],
        scratch_shapes=[pltpu.VMEM((tm, D), F32)],
        compiler_params=_params("parallel", "arbitrary"),
    )(x, xb, w1, w2, g, b)


def _rope_body(pos_ref, freq_ref, cos_ref, sin_ref):
    ang = pos_ref[...].astype(F32) * freq_ref[...]
    lane = lax.broadcasted_iota(jnp.int32, ang.shape, 1)
    keep = lane < QK_ROPE
    cos_ref[...] = jnp.where(keep, jnp.cos(ang), 0.0)
    sin_ref[...] = jnp.where(keep, jnp.sin(ang), 0.0)


def _rope_tables(pos, freq, tm=1024):
    T = pos.shape[0]
    shp = jax.ShapeDtypeStruct((T, LANE), F32)
    return pl.pallas_call(
        _rope_body,
        name="rope_tables",
        grid=(T // tm,),
        in_specs=[pl.BlockSpec((tm, 1), lambda i: (i, 0)),
                  pl.BlockSpec((1, LANE), lambda i: (0, 0))],
        out_specs=[pl.BlockSpec((tm, LANE), lambda i: (i, 0))] * 2,
        out_shape=[shp, shp],
        compiler_params=_params("parallel"),
    )(pos, freq)


def _qup_body(c_ref, g_ref, w_ref, cos_ref, sin_ref, o_ref, *, scale):
    cn = _rms_norm(c_ref[...], g_ref[...]).astype(BF16)
    r = _dot(cn, w_ref[...])
    cos = cos_ref[...] * scale
    sin = sin_ref[...] * scale
    hw = 2 * LANE
    rot0 = N_MIX_HEADS * hw
    for h in range(N_MIX_HEADS):
        o_ref[:, h * hw:h * hw + LANE] = (r[:, h * hw:h * hw + LANE] * scale).astype(o_ref.dtype)
        rope = r[:, h * hw + LANE:(h + 1) * hw] * cos + r[:, rot0 + h * LANE:rot0 + (h + 1) * LANE] * sin
        o_ref[:, h * hw + LANE:(h + 1) * hw] = rope.astype(o_ref.dtype)


def _q_up(h, g, w, cosp, sinp, scale, tm=512):
    T = h.shape[0]
    N = N_MIX_HEADS * 2 * LANE
    return pl.pallas_call(
        functools.partial(_qup_body, scale=scale),
        name="mla_q_up",
        grid=(T // tm,),
        in_specs=[pl.BlockSpec((tm, Q_LORA), lambda i: (i, 0)),
                  pl.BlockSpec((1, Q_LORA), lambda i: (0, 0)),
                  pl.BlockSpec(w.shape, lambda i: (0, 0)),
                  pl.BlockSpec((tm, LANE), lambda i: (i, 0)),
                  pl.BlockSpec((tm, LANE), lambda i: (i, 0))],
        out_specs=pl.BlockSpec((tm, N), lambda i: (i, 0)),
        out_shape=jax.ShapeDtypeStruct((T, N), BF16),
        compiler_params=_params("parallel"),
    )(h, g, w, cosp, sinp)


def _kvup_body(c_ref, kr_ref, g_ref, w_ref, cos_ref, sin_ref, kn_ref, v_ref, krp_ref):
    cn = _rms_norm(c_ref[...], g_ref[...]).astype(BF16)
    r = _dot(cn, w_ref[...])
    kn_ref[...] = r[:, :MIX_WIDTH].astype(kn_ref.dtype)
    v_ref[...] = r[:, MIX_WIDTH:].astype(v_ref.dtype)
    kr = kr_ref[...]
    swapped = pltpu.roll(kr, QK_ROPE, 1)
    krp_ref[...] = (kr * cos_ref[...] + swapped * sin_ref[...]).astype(krp_ref.dtype)


def _kv_up(h, h_tail, g, w, cosp, sinp, tm=512):
    T = h.shape[0]
    kr_blk = MEM_WIDTH // LANE
    return pl.pallas_call(
        _kvup_body,
        name="mla_kv_up",
        grid=(T // tm,),
        in_specs=[pl.BlockSpec((tm, KV_LORA), lambda i: (i, 1)),
                  pl.BlockSpec((tm, LANE), lambda i: (i, kr_blk)),
                  pl.BlockSpec((1, KV_LORA), lambda i: (0, 0)),
                  pl.BlockSpec(w.shape, lambda i: (0, 0)),
                  pl.BlockSpec((tm, LANE), lambda i: (i, 0)),
                  pl.BlockSpec((tm, LANE), lambda i: (i, 0))],
        out_specs=[pl.BlockSpec((tm, MIX_WIDTH), lambda i: (i, 0)),
                   pl.BlockSpec((tm, MIX_WIDTH), lambda i: (i, 0)),
                   pl.BlockSpec((tm, LANE), lambda i: (i, 0))],
        out_shape=[jax.ShapeDtypeStruct((T, MIX_WIDTH), BF16),
                   jax.ShapeDtypeStruct((T, MIX_WIDTH), BF16),
                   jax.ShapeDtypeStruct((T, LANE), BF16)],
        compiler_params=_params("parallel"),
    )(h, h_tail, g, w, cosp, sinp)


def _flash_body(q_ref, kn_ref, kr_ref, v_ref, o_ref, m_ref, l_ref, acc_ref, *, tq, nh):
    qi = pl.program_id(2)
    qw = 2 * LANE

    hs = range(nh)
    hd = [slice(h * HEAD_DIM, (h + 1) * HEAD_DIM) for h in hs]

    def scores(j):
        start = pl.multiple_of(j * tq, tq)
        kr = kr_ref[0, pl.ds(start, tq), :]
        ks = [jnp.concatenate([kn_ref[0, pl.ds(start, tq), hd[h]], kr], axis=1) for h in hs]
        return [_dot_nt(q_ref[0, :, h * qw:(h + 1) * qw], ks[h]) for h in hs]

    def update(js, ss):
        nb = range(len(js))
        starts = [pl.multiple_of(j * tq, tq) for j in js]
        rep = lambda col: jnp.broadcast_to(col, (tq, LANE))
        tile = lambda x: jnp.concatenate([x] * (tq // LANE), axis=1)
        m = [m_ref[h] for h in hs]
        m_new = list(m)
        for b in nb:
            m_new = [jnp.maximum(m_new[h], rep(jnp.max(ss[b][h], -1, keepdims=True))) for h in hs]
        a = [jnp.exp2(m[h] - m_new[h]) for h in hs]
        p = [[jnp.exp2(ss[b][h] - tile(m_new[h])) for h in hs] for b in nb]
        l = [a[h] * l_ref[h] for h in hs]
        acc = [a[h] * acc_ref[h] for h in hs]
        for b in nb:
            l = [l[h] + rep(jnp.sum(p[b][h], -1, keepdims=True)) for h in hs]
        pv = [[_dot(p[b][h].astype(BF16), v_ref[0, pl.ds(starts[b], tq), hd[h]]) for h in hs] for b in nb]
        for b in nb:
            acc = [acc[h] + pv[b][h] for h in hs]
        for h in hs:
            m_ref[h] = m_new[h]
            l_ref[h] = l[h]
            acc_ref[h] = acc[h]
        return l, acc

    m_ref[...] = jnp.full(m_ref.shape, -jnp.inf, F32)
    l_ref[...] = jnp.zeros(l_ref.shape, F32)
    acc_ref[...] = jnp.zeros(acc_ref.shape, F32)

    def pair(j):
        return [j, j + 1], [scores(j), scores(j + 1)]

    @pl.loop(0, qi // 4)
    def _(j4):
        first, second = pair(4 * j4), pair(4 * j4 + 2)
        update(*first)
        update(*second)

    @pl.when(qi % 4 >= 2)
    def _():
        update(*pair(4 * (qi // 4)))

    @pl.when(qi % 2 == 1)
    def _():
        update([qi - 1], [scores(qi - 1)])

    row = lax.broadcasted_iota(jnp.int32, (tq, tq), 0)
    col = lax.broadcasted_iota(jnp.int32, (tq, tq), 1)
    s = [jnp.where(col <= row, sh, -jnp.inf) for sh in scores(qi)]
    l, acc = update([qi], [s])
    o_ref[0] = jnp.concatenate([acc[h] / l[h] for h in hs], axis=1).astype(o_ref.dtype)


def _flash(q, kn, kr, v, tq=512, nh=2):
    B, S, _ = q.shape
    return pl.pallas_call(
        functools.partial(_flash_body, tq=tq, nh=nh),
        name="mla_flash",
        grid=(B, N_MIX_HEADS // nh, S // tq),
        in_specs=[pl.BlockSpec((1, tq, nh * 2 * LANE), lambda b, h, i: (b, i, h)),
                  pl.BlockSpec((1, S, nh * HEAD_DIM), lambda b, h, i: (b, 0, h)),
                  pl.BlockSpec((1, S, LANE), lambda b, h, i: (b, 0, 0)),
                  pl.BlockSpec((1, S, nh * HEAD_DIM), lambda b, h, i: (b, 0, h))],
        out_specs=pl.BlockSpec((1, tq, nh * HEAD_DIM), lambda b, h, i: (b, i, h)),
        out_shape=jax.ShapeDtypeStruct((B, S, MIX_WIDTH), BF16),
        scratch_shapes=[pltpu.VMEM((nh, tq, LANE), F32), pltpu.VMEM((nh, tq, LANE), F32),
                        pltpu.VMEM((nh, tq, HEAD_DIM), F32)],
        compiler_params=_params("parallel", "parallel", "arbitrary"),
    )(q, kn, kr, v)


def _memattn_body(q_ref, kv_ref, o_ref, *, scale):
    for h in range(N_MEM_HEADS):
        q = (q_ref[0, :, h * HEAD_DIM:(h + 1) * HEAD_DIM] * scale).astype(BF16)
        k = kv_ref[0, :, h * HEAD_DIM:(h + 1) * HEAD_DIM].astype(BF16)
        v = kv_ref[0, :, MEM_WIDTH + h * HEAD_DIM:MEM_WIDTH + (h + 1) * HEAD_DIM].astype(BF16)
        s = _dot_nt(q, k)
        p = jnp.exp(s - jnp.max(s, -1, keepdims=True))
        o = _dot(p.astype(BF16), v) / jnp.sum(p, -1, keepdims=True)
        o_ref[0, :, h * HEAD_DIM:(h + 1) * HEAD_DIM] = o.astype(o_ref.dtype)


def _mem_attention(h3, mem_kv, tm=512):
    B, S, _ = h3.shape
    M = mem_kv.shape[1]
    return pl.pallas_call(
        functools.partial(_memattn_body, scale=HEAD_DIM ** -0.5),
        name="mem_attention",
        grid=(B, S // tm),
        in_specs=[pl.BlockSpec((1, tm, MEM_WIDTH), lambda b, i: (b, i, 0)),
                  pl.BlockSpec((1, M, 2 * MEM_WIDTH), lambda b, i: (b, 0, 0))],
        out_specs=pl.BlockSpec((1, tm, MEM_WIDTH), lambda b, i: (b, i, 0)),
        out_shape=jax.ShapeDtypeStruct((B, S, MEM_WIDTH), BF16),
        compiler_params=_params("parallel", "parallel"),
    )(h3, mem_kv)


def _gate_body(ab_ref, alog_ref, dt_ref, kind_ref, o_ref):
    ab = ab_ref[...]
    kind = kind_ref[...]
    g = -jnp.exp(alog_ref[...]) * jax.nn.softplus(ab + dt_ref[...])
    o_ref[...] = jnp.where(kind == 1.0, g, jnp.where(kind == 2.0, jax.nn.sigmoid(ab), 0.0))


def _gdn_gates(h_tail, lane_params, tm=1024):
    T = h_tail.shape[0]
    W = MEM_WIDTH
    row = pl.BlockSpec((1, W), lambda i: (0, 0))
    return pl.pallas_call(
        _gate_body,
        name="gdn_gates",
        grid=(T // tm,),
        in_specs=[pl.BlockSpec((tm, W), lambda i: (i, 1)), row, row, row],
        out_specs=pl.BlockSpec((tm, W), lambda i: (i, 0)),
        out_shape=jax.ShapeDtypeStruct((T, W), F32),
        compiler_params=_params("parallel"),
    )(h_tail, *lane_params)


def _unit_lower_inverses(Ls, masks):
    eye, neg_strict16, blk32_only, blk64_only = masks
    idx = range(len(Ls))

    def bf(xs):
        return [x.astype(BF16) for x in xs]

    n1 = [L * neg_strict16 for L in Ls]
    nb = bf(n1)
    p = [eye + x for x in n1]
    for _ in range(3):
        nb = bf([_dot(nb[i], nb[i]) for i in idx])
        pb = bf(p)
        p = [p[i] + _dot(pb[i], nb[i]) for i in idx]
    for level in (blk32_only, blk64_only):
        pb = bf(p)
        cb = bf([L * level for L in Ls])
        t = bf([_dot(pb[i], cb[i]) for i in idx])
        p = [p[i] - _dot(t[i], pb[i]) for i in idx]
    return p


def _causal_conv_silu(x_ref, ln, prev, w):
    assert CONV_WIDTH == 4
    t0, t1, t2, t3 = (w[CONV_WIDTH - 1 - d:CONV_WIDTH - d, :] for d in range(CONV_WIDTH))
    xe = jnp.concatenate([prev, x_ref[0, :, ln]], axis=0)
    xs = pltpu.roll(xe, 1, 0)
    y = (xe * t0 + xs * t1) + pltpu.roll(xe * t2 + xs * t3, 2, 0)
    return _silu(y[SUBLANE:])


def _l2_normalize(x):
    return x * lax.rsqrt(jnp.sum(x * x, -1, keepdims=True) + L2_EPS)


def _gdn_body(q_ref, k_ref, v_ref, z_ref, qp_ref, kp_ref, vp_ref, wq_ref, wk_ref, wv_ref,
              gate_ref, onorm_ref, mtril_ref, masks_ref, o_ref, state_ref):
    R = q_ref.shape[1]
    nc = R // CHUNK
    step = pl.program_id(2)

    @pl.when(step == 0)
    def _():
        state_ref[...] = jnp.zeros_like(state_ref)

    not_first = (step > 0).astype(F32)
    row = lax.broadcasted_iota(jnp.int32, (R, R), 0)
    col = lax.broadcasted_iota(jnp.int32, (R, R), 1)

    tril = jnp.logical_and((row >> 6) == (col >> 6), col <= row)
    m_tril = mtril_ref[...]
    masks = tuple(masks_ref[i] for i in range(4))

    def split3(a):
        a1 = a.astype(BF16)
        r1 = a - a1.astype(F32)
        a2 = r1.astype(BF16)
        a3 = (r1 - a2.astype(F32)).astype(BF16)
        return a1, a2, a3

    def lane_bcast(a, lane):
        return jnp.broadcast_to(a[:, lane:lane + 1], (R, HEAD_DIM))

    gates = gate_ref[0]
    gparts = split3(gates)
    gc_all = _dot(m_tril, gparts[0]) + _dot(m_tril, gparts[1]) + _dot(m_tril, gparts[2])
    gc_rows = gc_all.T
    g_end_all = jnp.concatenate(
        [jnp.broadcast_to(gc_all[(c + 1) * CHUNK - 1:(c + 1) * CHUNK, :], (CHUNK, LANE)) for c in range(nc)], axis=0)
    egc_all = jnp.exp(gc_all)
    kdec_all = jnp.exp(g_end_all - gc_all)
    glast_all = jnp.exp(g_end_all)

    heads = range(GDN_HEADS)
    lns = [slice(hh * HEAD_DIM, (hh + 1) * HEAD_DIM) for hh in heads]
    q, k, v, beta, egc, decay, kb, kbf, lmat = [], [], [], [], [], [], [], [], []
    for hh, ln in zip(heads, lns):
        xq = _causal_conv_silu(q_ref, ln, qp_ref[0, :, ln] * not_first, wq_ref[:, ln])
        xk = _causal_conv_silu(k_ref, ln, kp_ref[0, :, ln] * not_first, wk_ref[:, ln])
        v.append(_causal_conv_silu(v_ref, ln, vp_ref[0, :, ln] * not_first, wv_ref[:, ln]))
        q.append(_l2_normalize(xq) * (HEAD_DIM ** -0.5))
        k.append(_l2_normalize(xk))
        gc = lane_bcast(gc_all, hh)
        beta.append(lane_bcast(gates, GDN_HEADS + hh))
        egc.append(lane_bcast(egc_all, hh))
        diff = jnp.concatenate([gc] * (R // HEAD_DIM), axis=1) - gc_rows[hh:hh + 1, :]
        decay.append(jnp.exp(jnp.where(tril, diff, MASKED_LOG)))
        kb.append(k[hh] * beta[hh])
        kbf.append(k[hh].astype(BF16))
        lmat.append(_dot_nt(kb[hh].astype(BF16), kbf[hh]) * decay[hh])

    tinv = _unit_lower_inverses(lmat, masks)

    u, w, a_qk, q_dec, k_dec = [], [], [], [], []
    for hh in heads:
        rhs = jnp.concatenate([v[hh] * beta[hh], kb[hh] * egc[hh]], axis=1).astype(BF16)
        sol = _dot(tinv[hh].astype(BF16), rhs)
        u.append(sol[:, :HEAD_DIM])
        w.append(sol[:, HEAD_DIM:].astype(BF16))
        a_qk.append((_dot_nt(q[hh].astype(BF16), kbf[hh]) * decay[hh]).astype(BF16))
        q_dec.append((q[hh] * egc[hh]).astype(BF16))
        k_dec.append((k[hh] * lane_bcast(kdec_all, hh)).astype(BF16))

    state = [state_ref[hh] for hh in heads]
    v_new = [[] for _ in heads]
    o_inter = [[] for _ in heads]
    for c in range(nc):
        rows = slice(c * CHUNK, (c + 1) * CHUNK)
        for hh in heads:
            both = _dot(jnp.concatenate([w[hh][rows], q_dec[hh][rows]], axis=0), state[hh].astype(BF16))
            vn = u[hh][rows] - both[:CHUNK]
            o_inter[hh].append(both[CHUNK:])
            g_last = glast_all[c * CHUNK:c * CHUNK + 1, hh:hh + 1]
            state[hh] = state[hh] * g_last + _dot_tn(k_dec[hh][rows], vn.astype(BF16))
            v_new[hh].append(vn)
    state_ref[...] = jnp.stack(state)

    outs = []
    for hh, ln in zip(heads, lns):
        vn_all = jnp.concatenate(v_new[hh], axis=0).astype(BF16)
        o = jnp.concatenate(o_inter[hh], axis=0) + _dot(a_qk[hh], vn_all)
        outs.append(_rms_norm(o, onorm_ref[...]) * _silu(z_ref[0, :, ln]))
    o_ref[0] = jnp.concatenate(outs, axis=1).astype(o_ref.dtype)


def _gdn_block_masks(R):
    row = lax.broadcasted_iota(jnp.int32, (R, R), 0)
    col = lax.broadcasted_iota(jnp.int32, (R, R), 1)

    def same(n):
        return (row // n) == (col // n)

    one = lambda c: jnp.where(c, 1.0, 0.0).astype(F32)
    tril = one(jnp.logical_and(same(CHUNK), col <= row)).astype(BF16)
    eye = one(row == col)
    neg_strict16 = -one(jnp.logical_and(same(16), col < row))
    blk32_only = one(jnp.logical_and(same(32), jnp.logical_not(same(16))))
    blk64_only = one(jnp.logical_and(same(CHUNK), jnp.logical_not(same(32))))
    return tril, jnp.stack([eye, neg_strict16, blk32_only, blk64_only])


def _gdn_mix(h3, conv_w, gates, o_norm):
    B, S, _ = h3.shape
    R, G = GDN_ROWS, GDN_HEADS
    W = G * HEAD_DIM
    ng = N_MIX_HEADS // G
    rb = R // SUBLANE

    def cur(off):
        return pl.BlockSpec((1, R, W), lambda b, g, i: (b, i, off * ng + g))

    def prev(off):
        return pl.BlockSpec((1, SUBLANE, W), lambda b, g, i: (b, jnp.maximum(i * rb - 1, 0), off * ng + g))

    def convw(off):
        return pl.BlockSpec((CONV_WIDTH, W), lambda b, g, i: (0, off * ng + g))

    return pl.pallas_call(
        _gdn_body,
        name="gdn_delta_rule",
        grid=(B, ng, S // R),
        in_specs=[cur(0), cur(1), cur(2), cur(3), prev(0), prev(1), prev(2),
                  convw(0), convw(1), convw(2),
                  pl.BlockSpec((1, R, LANE), lambda b, g, i: (b, i, g)),
                  pl.BlockSpec((1, HEAD_DIM), lambda b, g, i: (0, 0)),
                  pl.BlockSpec((R, R), lambda b, g, i: (0, 0)),
                  pl.BlockSpec((4, R, R), lambda b, g, i: (0, 0, 0))],
        out_specs=pl.BlockSpec((1, R, W), lambda b, g, i: (b, i, g)),
        out_shape=jax.ShapeDtypeStruct((B, S, MIX_WIDTH), BF16),
        scratch_shapes=[pltpu.VMEM((G, HEAD_DIM, HEAD_DIM), F32)],
        compiler_params=_params("parallel", "parallel", "arbitrary"),
    )(h3, h3, h3, h3, h3, h3, h3, conv_w, conv_w, conv_w, gates, o_norm, *_gdn_block_masks(R))


def _rotate_half_cols(w):
    half = w.shape[-1] // 2
    return jnp.concatenate([-w[..., half:], w[..., :half]], -1)


def _mla_weights(w_in, w_uq, w_ukv):
    k_r = w_in[:, Q_LORA + KV_LORA:Q_LORA + KV_LORA + QK_ROPE]
    q_mem = w_in[:, Q_LORA + KV_LORA + QK_ROPE:]
    w_tail = jnp.concatenate([q_mem, k_r, _rotate_half_cols(k_r)], 1).astype(BF16)

    uq = w_uq.reshape(Q_LORA, N_MIX_HEADS, QK_NOPE + QK_ROPE)
    nope, rope = uq[..., :QK_NOPE], uq[..., QK_NOPE:]
    zeros = jnp.zeros_like(rope)
    main = jnp.concatenate([nope, rope, zeros], -1).reshape(Q_LORA, -1)
    rot = jnp.concatenate([_rotate_half_cols(rope), zeros], -1).reshape(Q_LORA, -1)
    w_uq_p = jnp.concatenate([main, rot], 1).astype(BF16)

    ukv = w_ukv.reshape(KV_LORA, N_MIX_HEADS, QK_NOPE + HEAD_DIM)
    w_ukv_p = jnp.concatenate([ukv[..., :QK_NOPE].reshape(KV_LORA, -1),
                               ukv[..., QK_NOPE:].reshape(KV_LORA, -1)], 1).astype(BF16)
    return w_tail, w_uq_p, w_ukv_p


def _group_lanes(a, b):
    G = GDN_HEADS
    lead = a.shape[:-1]
    a = a.reshape(lead + (N_MIX_HEADS // G, G))
    b = b.reshape(lead + (N_MIX_HEADS // G, G))
    pad = jnp.zeros(lead + (N_MIX_HEADS // G, LANE - 2 * G), a.dtype)
    return jnp.concatenate([a, b, pad], -1).reshape(lead + (-1,))


def _gdn_in_weights(w_in, a_log, dt_bias):
    D = w_in.shape[0]
    a = w_in[:, 4 * MIX_WIDTH:4 * MIX_WIDTH + N_MIX_HEADS]
    b = w_in[:, 4 * MIX_WIDTH + N_MIX_HEADS:4 * MIX_WIDTH + 2 * N_MIX_HEADS]
    q_mem = w_in[:, 4 * MIX_WIDTH + 2 * N_MIX_HEADS:]
    gate_w = _group_lanes(a, b)
    pad = jnp.zeros((D, MEM_WIDTH - gate_w.shape[1]), w_in.dtype)
    w_tail = jnp.concatenate([q_mem, gate_w, pad], 1).astype(BF16)

    def lanes(va, vb):
        v = _group_lanes(va, vb)
        return jnp.pad(v, (0, MEM_WIDTH - v.shape[0]))[None, :]

    zero, one = jnp.zeros_like(a_log), jnp.ones_like(a_log)
    return w_tail, (lanes(a_log, zero), lanes(dt_bias, zero), lanes(one, 2.0 * one))


def _pad_lanes(v):
    return jnp.pad(v, (0, LANE - v.shape[0]))[None, :]


def kernel(x, mem, positions, mla_w_in, mla_q_norm, mla_w_uq, mla_kv_norm, mla_w_ukv, gdn_w_in, gdn_conv, gdn_a_log, gdn_dt_bias, gdn_o_norm, mem_w_kv, w_out, ln1_g, ln1_b, mlp_w1, mlp_w2, ln2_g, ln2_b):
    B, S, D = x.shape
    T = B * S
    depth = w_out.shape[0]
    alpha = (2 * depth) ** 0.25

    inv_freq = 1.0 / (ROPE_THETA ** (jnp.arange(0, QK_ROPE, 2, dtype=F32) / QK_ROPE))
    freq = _pad_lanes(jnp.concatenate([inv_freq, inv_freq]))
    cosp, sinp = _rope_tables(positions.reshape(T, 1), freq)

    xf = x.reshape(T, D)
    xb = xf
    mem2 = mem.reshape(B * mem.shape[1], D)
    mem_w_kv_b, w_out_b = mem_w_kv.astype(BF16), w_out.astype(BF16)
    mlp_w1_b, mlp_w2_b = mlp_w1.astype(BF16), mlp_w2.astype(BF16)
    mla_w_in_b, gdn_w_in_b = mla_w_in.astype(BF16), gdn_w_in.astype(BF16)
    for i in range(depth):
        j = i // 2
        mem_kv = _matmul(mem2, mem_w_kv_b, F32, tm=256, tn=512, layer=i, n_cols=2 * MEM_WIDTH)
        mem_kv = mem_kv.reshape(B, mem.shape[1], 2 * MEM_WIDTH)
        if i % 2 == 0:
            w_tail, w_uq_p, w_ukv_p = _mla_weights(mla_w_in[j], mla_w_uq[j], mla_w_ukv[j])
            h = _matmul(xb, mla_w_in_b, F32, tm=1024, tn=Q_LORA + KV_LORA, layer=j, n_cols=Q_LORA + KV_LORA)
            h_tail = _matmul(xb, w_tail, F32, tm=1024, tn=w_tail.shape[1])
            q = _q_up(h, mla_q_norm[j][None, :], w_uq_p, cosp, sinp, (QK_NOPE + QK_ROPE) ** -0.5 * LOG2_E)
            kn, v, krp = _kv_up(h, h_tail, mla_kv_norm[j][None, :], w_ukv_p, cosp, sinp)
            mix = _flash(q.reshape(B, S, -1), kn.reshape(B, S, -1), krp.reshape(B, S, -1),
                         v.reshape(B, S, -1)).reshape(T, MIX_WIDTH)
        else:
            w_tail, lane_params = _gdn_in_weights(gdn_w_in[j], gdn_a_log[j], gdn_dt_bias[j])
            h = _matmul(xb, gdn_w_in_b, F32, tm=1024, tn=MIX_WIDTH, layer=j, n_cols=4 * MIX_WIDTH)
            h_tail = _matmul(xb, w_tail, F32, tm=1024, tn=w_tail.shape[1])
            gates = _gdn_gates(h_tail, lane_params)
            mix = _gdn_mix(h.reshape(B, S, -1), gdn_conv[j], gates.reshape(B, S, -1),
                           gdn_o_norm[j][None, :]).reshape(T, MIX_WIDTH)
        mem_o = _mem_attention(h_tail.reshape(B, S, -1), mem_kv)
        xf, xb = _outproj_ln(mix, mem_o.reshape(T, MEM_WIDTH), w_out_b, i, xf,
                             ln1_g[i][None, :], ln1_b[i][None, :], alpha)
        xf, xb = _mlp_ln(xf, xb, mlp_w1_b, mlp_w2_b, i, ln2_g[i][None, :], ln2_b[i][None, :], alpha)
    return xf.reshape(B, S, D)
```
